```python
import math
import jax, jax.numpy as jnp
from jax import lax
import numpy as np

D_MODEL = 1024
BATCH = 8
SEQ = 4096
DEPTH = 2
DEC_BATCH = 32
DEC_SEQ = 4
PAST_LEN = 16384
PAGE_SIZE = 128

N_A_LAYERS = DEPTH // 2
N_B_LAYERS = DEPTH - N_A_LAYERS
SSM_GROUP_CH = 16
SSM_GROUPS = D_MODEL // SSM_GROUP_CH
SSM_STATE = 64
SSM_CHUNK = 128
DT_MIN = 1e-3
DT_MAX = 1e-1
HEAD_DIM = 64
N_HEADS = D_MODEL // (2 * HEAD_DIM)
V_DIM = 2 * HEAD_DIM
Q_BLOCK = 128
N_EGROUPS = 4
EXPERTS_PER_GROUP = 8
N_EXPERTS = N_EGROUPS * EXPERTS_PER_GROUP
TOP_K_WITHIN = 2
D_EXPERT = D_MODEL // 2
MOE_BLOCK = 128
RMS_EPS = 1e-6
NEG_INF = -1e30

kernel_name = "yoco_s5_diffattn_hmoe_step"


def rms_norm(x, w):
    x32 = x.astype(jnp.float32)
    return x32 * lax.rsqrt(jnp.mean(x32 * x32, axis=-1, keepdims=True) + RMS_EPS) * w.astype(jnp.float32)


def modulate(xn, shift, scale):
    return xn * (1.0 + scale[:, None, :]) + shift[:, None, :]


def s5_discretise(a_re, a_im, log_dt, b_re, b_im):
    f32 = jnp.float32
    lam = lax.complex(a_re.astype(f32), a_im.astype(f32))
    dt = jnp.exp(log_dt.astype(f32))
    lam_bar = jnp.exp(lam * dt)
    b = lax.complex(b_re.astype(f32), b_im.astype(f32))
    b_bar = ((lam_bar - 1.0) / lam)[..., None] * b
    return lam_bar, b_bar


def s5_scan(u, h0, lam_bar, b_bar, c_cplx):
    bt, L, _ = u.shape
    chunk = math.gcd(L, SSM_CHUNK)
    n_chunks = L // chunk
    ug = u.reshape(bt, n_chunks, chunk, SSM_GROUPS, SSM_GROUP_CH).transpose(1, 2, 0, 3, 4)

    def combine(e1, e2):
        a1, b1 = e1
        a2, b2 = e2
        return a2 * a1, a2 * b1 + b2

    def step(h, u_c):
        bu = jnp.einsum('gpc,tbgc->tbgp', b_bar, u_c)
        bu = bu.at[0].add(lam_bar * h)
        a = jnp.broadcast_to(lam_bar, bu.shape)
        _, hs = lax.associative_scan(combine, (a, bu), axis=0)
        y = jnp.real(jnp.einsum('gcp,tbgp->tbgc', c_cplx, hs))
        return hs[-1], y

    h_last, ys = lax.scan(step, h0, ug)
    y = ys.transpose(2, 0, 1, 3, 4).reshape(bt, L, D_MODEL)
    return y, h_last


def s5_mixer(xm, h0, w_in, a_re, a_im, log_dt, b_re, b_im, c_re, c_im, d_skip, w_glu):
    f32 = jnp.float32
    u = jnp.einsum('bld,de->ble', xm, w_in).astype(f32)
    lam_bar, b_bar = s5_discretise(a_re, a_im, log_dt, b_re, b_im)
    c_cplx = lax.complex(c_re.astype(f32), c_im.astype(f32))
    y, h_last = s5_scan(u, h0, lam_bar, b_bar, c_cplx)
    g = jax.nn.gelu(y + d_skip.astype(f32) * u)
    vg = g @ w_glu
    out = vg[..., :D_MODEL] * jax.nn.sigmoid(vg[..., D_MODEL:])
    return out, h_last


def routed_experts(xt, experts, gates, w1, w3, w2):
    T, D = xt.shape
    K = experts.shape[1]
    A = T * K
    flat_e = experts.reshape(-1).astype(jnp.int32)
    order = jnp.argsort(flat_e)
    sorted_e = flat_e[order]
    counts = jnp.bincount(flat_e, length=N_EXPERTS).astype(jnp.int32)
    padded = (counts + MOE_BLOCK - 1) // MOE_BLOCK * MOE_BLOCK
    pad_end = jnp.cumsum(padded)
    pad_start = pad_end - padded
    start = jnp.cumsum(counts) - counts
    dest_sorted = (pad_start[sorted_e] + (jnp.arange(A, dtype=jnp.int32) - start[sorted_e])).astype(jnp.int32)
    n_blocks = (A + MOE_BLOCK - 1) // MOE_BLOCK + N_EXPERTS
    n_rows = n_blocks * MOE_BLOCK
    token_of_row = jnp.full((n_rows,), T, jnp.int32).at[dest_sorted].set((order // K).astype(jnp.int32))
    block_expert = jnp.minimum(
        jnp.searchsorted(pad_end, jnp.arange(n_blocks, dtype=jnp.int32) * MOE_BLOCK, side='right'),
        N_EXPERTS - 1).astype(jnp.int32)
    x_pad = jnp.concatenate([xt, jnp.zeros((1, D), xt.dtype)], axis=0)
    xb = x_pad[token_of_row].reshape(n_blocks, MOE_BLOCK, D)

    def expert_block(args):
        xblk, e = args
        h = jax.nn.silu(xblk @ w1[e]) * (xblk @ w3[e])
        return (h @ w2[e]).astype(jnp.float32)

    yb = lax.map(expert_block, (xb, block_expert)).reshape(n_rows, D)
    dest = jnp.zeros((A,), jnp.int32).at[order].set(dest_sorted)
    return jnp.sum(yb[dest].reshape(T, K, D) * gates[..., None], axis=1)


def hier_moe(x, w_group, b_group, w_erouter, b_erouter, w1, w3, w2):
    bt, L, D = x.shape
    xt = x.reshape(-1, D)
    g_prob = jax.nn.softmax((xt @ w_group + b_group).astype(jnp.float32), axis=-1)
    g_sel = jnp.argmax(g_prob, axis=-1).astype(jnp.int32)
    g_gate = jnp.take_along_axis(g_prob, g_sel[:, None], axis=-1)
    e_logits = jnp.einsum('td,gde->tge', xt, w_erouter) + b_erouter
    e_logits = jnp.take_along_axis(e_logits, g_sel[:, None, None], axis=1)[:, 0].astype(jnp.float32)
    e_prob = jax.nn.softmax(e_logits, axis=-1)
    top_p, top_i = lax.top_k(e_prob, TOP_K_WITHIN)
    top_p = top_p / jnp.sum(top_p, axis=-1, keepdims=True)
    gates = g_gate * top_p
    experts = g_sel[:, None] * EXPERTS_PER_GROUP + top_i
    return routed_experts(xt, experts, gates, w1, w3, w2).reshape(bt, L, D)


def shared_kv(h, shift, scale, norm_kv_w, w_kv, k_norm_w):
    bt, L, _ = h.shape
    hn = modulate(rms_norm(h, norm_kv_w), shift, scale)
    kv = hn @ w_kv
    k = rms_norm(kv[..., :D_MODEL].reshape(bt, L, N_HEADS, 2, HEAD_DIM), k_norm_w).reshape(bt, L, N_HEADS, V_DIM)
    v = kv[..., D_MODEL:].reshape(bt, L, N_HEADS, V_DIM).astype(jnp.float32)
    return k, v


def masked_softmax(s, mask):
    return jax.nn.softmax(jnp.where(mask, s.astype(jnp.float32), NEG_INF), axis=-1)


def diff_attn_prompt(q1, q2, k, v, lam):
    bt, L = q1.shape[:2]
    qb = math.gcd(L, Q_BLOCK)
    nb = L // qb
    k1, k2 = k[..., :HEAD_DIM], k[..., HEAD_DIM:]
    kpos = jnp.arange(L)

    def to_blocks(q):
        return q.reshape(bt, nb, qb, N_HEADS, HEAD_DIM).transpose(1, 0, 2, 3, 4)

    def block(args):
        i, q1b, q2b = args
        qpos = i * qb + jnp.arange(qb)
        mask = kpos[None, :] <= qpos[:, None]
        p1 = masked_softmax(jnp.einsum('bqhd,bkhd->bhqk', q1b, k1), mask)
        p2 = masked_softmax(jnp.einsum('bqhd,bkhd->bhqk', q2b, k2), mask)
        return jnp.einsum('bhqk,bkhe->bqhe', p1 - lam * p2, v)

    o = lax.map(block, (jnp.arange(nb), to_blocks(q1), to_blocks(q2)))
    return o.transpose(1, 0, 2, 3, 4).reshape(bt, L, N_HEADS, V_DIM)


def diff_attn_sample(q1, q2, k_new, v_new, cache_k, cache_v, page_table, lam):
    bd, T = q1.shape[:2]
    f32 = jnp.float32

    def scores(kk):
        s1 = jnp.einsum('bqhd,bkhd->bhqk', q1, kk[..., :HEAD_DIM]).astype(f32)
        s2 = jnp.einsum('bqhd,bkhd->bhqk', q2, kk[..., HEAD_DIM:]).astype(f32)
        return s1, s2

    def update(stats, s, vals):
        m, l, acc = stats
        m_new = jnp.maximum(m, jnp.max(s, axis=-1))
        corr = jnp.exp(m - m_new)
        p = jnp.exp(s - m_new[..., None])
        l = l * corr + jnp.sum(p, axis=-1)
        acc = acc * corr[..., None] + jnp.einsum('bhqk,bkhe->bhqe', p, vals).astype(f32)
        return (m_new, l, acc)

    init = (jnp.full((bd, N_HEADS, T), NEG_INF, f32), jnp.zeros((bd, N_HEADS, T), f32),
            jnp.zeros((bd, N_HEADS, T, V_DIM), f32))

    def page_step(carry, pages):
        st1, st2 = carry
        kp = cache_k[pages]
        vp = cache_v[pages]
        s1, s2 = scores(kp)
        return (update(st1, s1, vp), update(st2, s2, vp)), None

    (st1, st2), _ = lax.scan(page_step, (init, init), page_table.T)
    causal = jnp.tril(jnp.ones((T, T), dtype=bool))
    s1, s2 = scores(k_new)
    st1 = update(st1, jnp.where(causal, s1, NEG_INF), v_new)
    st2 = update(st2, jnp.where(causal, s2, NEG_INF), v_new)
    o = st1[2] / st1[1][..., None] - lam * (st2[2] / st2[1][..., None])
    return o.transpose(0, 2, 1, 3)


def diff_attn_layer(xm, k, v, kv_past, w_q, qn_w, lq1, lk1, lq2, lk2, sub_w, w_o, layer_index):
    f32 = jnp.float32
    bt, L, _ = xm.shape
    q = (xm @ w_q).reshape(bt, L, N_HEADS, 2, HEAD_DIM)
    q = rms_norm(q, qn_w) * (HEAD_DIM ** -0.5)
    q1, q2 = q[..., 0, :], q[..., 1, :]
    lam_init = 0.8 - 0.6 * math.exp(-0.3 * layer_index)
    lam = (jnp.exp(jnp.sum(lq1.astype(f32) * lk1.astype(f32)))
           - jnp.exp(jnp.sum(lq2.astype(f32) * lk2.astype(f32))) + lam_init)
    if kv_past is None:
        o = diff_attn_prompt(q1, q2, k, v, lam)
    else:
        o = diff_attn_sample(q1, q2, k, v, kv_past[0], kv_past[1], kv_past[2], lam)
    o = rms_norm(o, sub_w) * (1.0 - lam_init)
    return o.reshape(bt, L, D_MODEL) @ w_o


def run_trunk(x, c, ssm_h0_re, ssm_h0_im, kv_past, p):
    f32 = jnp.float32
    h = x.astype(f32)
    c_act = jax.nn.silu(c.astype(f32))
    new_re, new_im = [], []
    k = None
    v = None
    for l in range(DEPTH):
        if l == N_A_LAYERS:
            mod_kv = c_act @ p['w_mod_kv'] + p['b_mod_kv']
            k, v = shared_kv(h, mod_kv[:, :D_MODEL], mod_kv[:, D_MODEL:], p['norm_kv_w'], p['w_kv'], p['k_norm_w'])
        mod = c_act @ p['w_mod'][l] + p['b_mod'][l]
        sh1, sc1, g1, sh2, sc2, g2 = jnp.split(mod, 6, axis=-1)
        xm = modulate(rms_norm(h, p['norm_mix_w'][l]), sh1, sc1)
        if l < N_A_LAYERS:
            h0 = lax.complex(ssm_h0_re[l].astype(f32), ssm_h0_im[l].astype(f32))
            out, h_last = s5_mixer(xm, h0, p['ssm_w_in'][l], p['ssm_a_re'][l], p['ssm_a_im'][l], p['ssm_log_dt'][l],
                                   p['ssm_b_re'][l], p['ssm_b_im'][l], p['ssm_c_re'][l], p['ssm_c_im'][l],
                                   p['ssm_d_skip'][l], p['ssm_w_glu'][l])
            new_re.append(jnp.real(h_last))
            new_im.append(jnp.imag(h_last))
        else:
            j = l - N_A_LAYERS
            out = diff_attn_layer(xm, k, v, kv_past, p['attn_w_q'][j], p['q_norm_w'][j], p['lambda_q1'][j],
                                  p['lambda_k1'][j], p['lambda_q2'][j], p['lambda_k2'][j], p['subln_w'][j],
                                  p['attn_w_o'][j], l)
        h = h + g1[:, None, :] * out
        xf = modulate(rms_norm(h, p['norm_ffn_w'][l]), sh2, sc2)
        h = h + g2[:, None, :] * hier_moe(xf, p['moe_w_group'][l], p['moe_b_group'][l], p['moe_w_erouter'][l],
                                          p['moe_b_erouter'][l], p['moe_w1'][l], p['moe_w3'][l], p['moe_w2'][l])
    return h, k, v, jnp.stack(new_re), jnp.stack(new_im)


def setup_inputs(seed: int = 0) -> dict:
    key = jax.random.key(seed)
    ks = iter(jax.random.split(key, 64))
    f32 = jnp.float32
    D = D_MODEL

    def nrm(shape, s):
        return s * jax.random.normal(next(ks), shape, f32)

    n_pages = PAST_LEN // PAGE_SIZE
    n_phys = (DEC_BATCH * n_pages * 5) // 4
    inp = {}
    inp['x_prompt'] = nrm((BATCH, SEQ, D), 1.0)
    inp['x_sample'] = nrm((DEC_BATCH, DEC_SEQ, D), 1.0)
    inp['cache_k'] = nrm((n_phys, PAGE_SIZE, N_HEADS, V_DIM), 1.0)
    inp['cache_v'] = nrm((n_phys, PAGE_SIZE, N_HEADS, V_DIM), 1.0)
    inp['page_table'] = jax.random.permutation(next(ks), n_phys)[: DEC_BATCH * n_pages].reshape(
        DEC_BATCH, n_pages).astype(jnp.int32)
    inp['state_ssm_re'] = nrm((N_A_LAYERS, DEC_BATCH, SSM_GROUPS, SSM_STATE), 0.5)
    inp['state_ssm_im'] = nrm((N_A_LAYERS, DEC_BATCH, SSM_GROUPS, SSM_STATE), 0.5)
    inp['c_prompt'] = nrm((BATCH, D), 1.0)
    inp['c_sample'] = nrm((DEC_BATCH, D), 1.0)
    inp['w_mod'] = nrm((DEPTH, D, 6 * D), D ** -0.5)
    inp['b_mod'] = nrm((DEPTH, 6 * D), 0.02)
    inp['norm_mix_w'] = 1.0 + nrm((DEPTH, D), 0.02)
    inp['norm_ffn_w'] = 1.0 + nrm((DEPTH, D), 0.02)
    inp['ssm_w_in'] = nrm((N_A_LAYERS, D, D), D ** -0.5)
    inp['ssm_a_re'] = -0.5 + nrm((N_A_LAYERS, SSM_GROUPS, SSM_STATE), 0.01)
    inp['ssm_a_im'] = math.pi * jnp.arange(SSM_STATE, dtype=f32) + nrm((N_A_LAYERS, SSM_GROUPS, SSM_STATE), 0.01)
    inp['ssm_log_dt'] = jax.random.uniform(next(ks), (N_A_LAYERS, SSM_GROUPS, SSM_STATE), f32,
                                           minval=math.log(DT_MIN), maxval=math.log(DT_MAX))
    inp['ssm_b_re'] = nrm((N_A_LAYERS, SSM_GROUPS, SSM_STATE, SSM_GROUP_CH), (2 * SSM_GROUP_CH) ** -0.5)
    inp['ssm_b_im'] = nrm((N_A_LAYERS, SSM_GROUPS, SSM_STATE, SSM_GROUP_CH), (2 * SSM_GROUP_CH) ** -0.5)
    inp['ssm_c_re'] = nrm((N_A_LAYERS, SSM_GROUPS, SSM_GROUP_CH, SSM_STATE), (2 * SSM_STATE) ** -0.5)
    inp['ssm_c_im'] = nrm((N_A_LAYERS, SSM_GROUPS, SSM_GROUP_CH, SSM_STATE), (2 * SSM_STATE) ** -0.5)
    inp['ssm_d_skip'] = nrm((N_A_LAYERS, D), 0.5)
    inp['ssm_w_glu'] = nrm((N_A_LAYERS, D, 2 * D), D ** -0.5)
    inp['w_mod_kv'] = nrm((D, 2 * D), D ** -0.5)
    inp['b_mod_kv'] = nrm((2 * D,), 0.02)
    inp['norm_kv_w'] = 1.0 + nrm((D,), 0.02)
    inp['w_kv'] = nrm((D, 2 * D), D ** -0.5)
    inp['k_norm_w'] = 1.0 + nrm((HEAD_DIM,), 0.02)
    inp['attn_w_q'] = nrm((N_B_LAYERS, D, D), D ** -0.5)
    inp['q_norm_w'] = 1.0 + nrm((N_B_LAYERS, HEAD_DIM), 0.02)
    inp['lambda_q1'] = nrm((N_B_LAYERS, HEAD_DIM), 0.1)
    inp['lambda_k1'] = nrm((N_B_LAYERS, HEAD_DIM), 0.1)
    inp['lambda_q2'] = nrm((N_B_LAYERS, HEAD_DIM), 0.1)
    inp['lambda_k2'] = nrm((N_B_LAYERS, HEAD_DIM), 0.1)
    inp['subln_w'] = 1.0 + nrm((N_B_LAYERS, V_DIM), 0.02)
    inp['attn_w_o'] = nrm((N_B_LAYERS, D, D), D ** -0.5)
    inp['moe_w_group'] = nrm((DEPTH, D, N_EGROUPS), D ** -0.5)
    inp['moe_b_group'] = nrm((DEPTH, N_EGROUPS), 0.01)
    inp['moe_w_erouter'] = nrm((DEPTH, N_EGROUPS, D, EXPERTS_PER_GROUP), D ** -0.5)
    inp['moe_b_erouter'] = nrm((DEPTH, N_EGROUPS, EXPERTS_PER_GROUP), 0.01)
    inp['moe_w1'] = nrm((DEPTH, N_EXPERTS, D, D_EXPERT), D ** -0.5)
    inp['moe_w3'] = nrm((DEPTH, N_EXPERTS, D, D_EXPERT), D ** -0.5)
    inp['moe_w2'] = nrm((DEPTH, N_EXPERTS, D_EXPERT, D), D_EXPERT ** -0.5)
    return inp


def reference(x_prompt, x_sample, cache_k, cache_v, page_table, state_ssm_re, state_ssm_im, c_prompt, c_sample,
              w_mod, b_mod, norm_mix_w, norm_ffn_w,
              ssm_w_in, ssm_a_re, ssm_a_im, ssm_log_dt, ssm_b_re, ssm_b_im, ssm_c_re, ssm_c_im, ssm_d_skip, ssm_w_glu,
              w_mod_kv, b_mod_kv, norm_kv_w, w_kv, k_norm_w,
              attn_w_q, q_norm_w, lambda_q1, lambda_k1, lambda_q2, lambda_k2, subln_w, attn_w_o,
              moe_w_group, moe_b_group, moe_w_erouter, moe_b_erouter, moe_w1, moe_w3, moe_w2):
    p = dict(w_mod=w_mod, b_mod=b_mod, norm_mix_w=norm_mix_w, norm_ffn_w=norm_ffn_w,
             ssm_w_in=ssm_w_in, ssm_a_re=ssm_a_re, ssm_a_im=ssm_a_im, ssm_log_dt=ssm_log_dt,
             ssm_b_re=ssm_b_re, ssm_b_im=ssm_b_im, ssm_c_re=ssm_c_re, ssm_c_im=ssm_c_im,
             ssm_d_skip=ssm_d_skip, ssm_w_glu=ssm_w_glu,
             w_mod_kv=w_mod_kv, b_mod_kv=b_mod_kv, norm_kv_w=norm_kv_w, w_kv=w_kv, k_norm_w=k_norm_w,
             attn_w_q=attn_w_q, q_norm_w=q_norm_w, lambda_q1=lambda_q1, lambda_k1=lambda_k1,
             lambda_q2=lambda_q2, lambda_k2=lambda_k2, subln_w=subln_w, attn_w_o=attn_w_o,
             moe_w_group=moe_w_group, moe_b_group=moe_b_group, moe_w_erouter=moe_w_erouter,
             moe_b_erouter=moe_b_erouter, moe_w1=moe_w1, moe_w3=moe_w3, moe_w2=moe_w2)
    zero_state = jnp.zeros((N_A_LAYERS, x_prompt.shape[0], SSM_GROUPS, SSM_STATE), jnp.float32)
    y_prompt, k_prompt, v_prompt, ssm_re_prompt, ssm_im_prompt = run_trunk(
        x_prompt, c_prompt, zero_state, zero_state, None, p)
    y_sample, k_sample, v_sample, ssm_re_sample, ssm_im_sample = run_trunk(
        x_sample, c_sample, state_ssm_re, state_ssm_im, (cache_k, cache_v, page_table), p)
    return (y_prompt, y_sample, k_prompt, v_prompt, k_sample, v_sample,
            ssm_re_prompt, ssm_im_prompt, ssm_re_sample, ssm_im_sample)
```

```python
import functools
import math

import jax
import jax.numpy as jnp
from jax import lax
from jax.experimental import pallas as pl
from jax.experimental.pallas import tpu as pltpu

F32 = jnp.float32
BF16 = jnp.bfloat16
HIGHEST = lax.Precision.HIGHEST

SSM_GROUP_CH = 16
HEAD_DIM = 64
V_DIM = 2 * HEAD_DIM
N_EGROUPS = 4
EXPERTS_PER_GROUP = 8
N_EXPERTS = N_EGROUPS * EXPERTS_PER_GROUP
RMS_EPS = 1e-6
NEG_INF = -1e30

LANES = 128
SUBLANES = 8
MXU_DIM = 256

S5_CHUNK = MXU_DIM // SSM_GROUP_CH
ROW_TILE = 512
ATTN_TILE = 512
PAGES_PER_STEP = 4
ROUTE_LANES = LANES
VMEM_LIMIT = 48 * 1024 * 1024


def _params(*sem):
    return pltpu.CompilerParams(dimension_semantics=sem, vmem_limit_bytes=VMEM_LIMIT)


def _norm_mod(x, nw, shift, scale):
    ms = jnp.mean(x * x, axis=-1, keepdims=True)
    return x * lax.rsqrt(ms + RMS_EPS) * nw * (1.0 + scale) + shift


def _half_norm(x, w, scale):
    lane = lax.broadcasted_iota(jnp.int32, x.shape, 1)
    low = lane < HEAD_DIM
    sq = x * x
    tot = jnp.sum(sq, axis=-1, keepdims=True)
    lo = jnp.sum(jnp.where(low, sq, 0.0), axis=-1, keepdims=True)
    ms = jnp.where(low, lo, tot - lo) * (1.0 / HEAD_DIM)
    return x * lax.rsqrt(ms + RMS_EPS) * w * scale


class _Group:
    def __init__(self, bt, seq):
        self.bt, self.seq, self.t = bt, seq, bt * seq
        if seq % ROW_TILE == 0:
            self.tm, self.per_batch = ROW_TILE, True
        else:
            assert self.t <= ROW_TILE and self.t % SUBLANES == 0
            self.tm, self.per_batch = self.t, False
        self.steps = self.t // self.tm

    def mod(self, vec):
        d = vec.shape[-1]
        if self.per_batch:
            per = self.seq // self.tm
            return vec[:, None, :], pl.BlockSpec((1, 1, d), lambda i: (i // per, 0, 0))
        arr = jnp.repeat(vec, self.seq, axis=0).reshape(self.steps, self.tm, d)
        return arr, pl.BlockSpec((1, self.tm, d), lambda i: (i, 0, 0))

    def rows(self, d):
        return pl.BlockSpec((self.tm, d), lambda i: (i, 0))


def _const(shape):
    return pl.BlockSpec(shape, lambda i: (0,) * len(shape))


def _mod_kernel(c_ref, w_ref, b_ref, o_ref):
    c = c_ref[...]
    a = c * jax.nn.sigmoid(c)
    o_ref[...] = jnp.dot(a, w_ref[...], preferred_element_type=F32, precision=HIGHEST) + b_ref[...]


def _mod_linear(c, w, b):
    m, d = c.shape
    n = w.shape[1]
    tn = 1024
    return pl.pallas_call(
        _mod_kernel,
        out_shape=jax.ShapeDtypeStruct((m, n), F32),
        grid=(n // tn,),
        in_specs=[pl.BlockSpec((m, d), lambda j: (0, 0)),
                  pl.BlockSpec((d, tn), lambda j: (0, j)),
                  pl.BlockSpec((1, tn), lambda j: (0, j))],
        out_specs=pl.BlockSpec((m, tn), lambda j: (0, j)),
        compiler_params=_params("parallel"),
        name="adaln_mod",
    )(c, w, b.reshape(1, n))


def _norm_linear_kernel(x_ref, nw_ref, sh_ref, sc_ref, w_ref, o_ref):
    xm = _norm_mod(x_ref[...], nw_ref[...], sh_ref[0], sc_ref[0])
    o_ref[...] = jnp.dot(xm.astype(BF16), w_ref[...], preferred_element_type=F32).astype(o_ref.dtype)


def _norm_linear(grp, h, nw, shift, scale, w_bf, name):
    d, n = w_bf.shape
    sh, sh_spec = grp.mod(shift)
    sc, sc_spec = grp.mod(scale)
    return pl.pallas_call(
        _norm_linear_kernel,
        out_shape=jax.ShapeDtypeStruct((grp.t, n), BF16),
        grid=(grp.steps,),
        in_specs=[grp.rows(d), _const((1, d)), sh_spec, sc_spec, _const((d, n))],
        out_specs=grp.rows(n),
        compiler_params=_params("parallel"),
        name=name,
    )(h, nw.reshape(1, d), sh, sc, w_bf)


def _kv_kernel(x_ref, nw_ref, sh_ref, sc_ref, w_ref, knw_ref, k_ref, v_ref, kb_ref, vb_ref):
    d = x_ref.shape[-1]
    xm = _norm_mod(x_ref[...], nw_ref[...], sh_ref[0], sc_ref[0])
    kv = jnp.dot(xm.astype(BF16), w_ref[...], preferred_element_type=F32)
    v = kv[:, d:]
    v_ref[...] = v
    vb_ref[...] = v.astype(BF16)
    for hd in range(d // V_DIM):
        sl = slice(hd * V_DIM, (hd + 1) * V_DIM)
        kn = _half_norm(kv[:, sl], knw_ref[...], 1.0)
        k_ref[:, sl] = kn
        kb_ref[:, sl] = kn.astype(BF16)


def _kv_proj(grp, h, nw, shift, scale, w_bf, k_norm_w):
    d = h.shape[-1]
    sh, sh_spec = grp.mod(shift)
    sc, sc_spec = grp.mod(scale)
    knw = jnp.tile(k_norm_w.astype(F32), V_DIM // HEAD_DIM).reshape(1, V_DIM)
    out = jax.ShapeDtypeStruct((grp.t, d), F32)
    outb = jax.ShapeDtypeStruct((grp.t, d), BF16)
    return pl.pallas_call(
        _kv_kernel,
        out_shape=(out, out, outb, outb),
        grid=(grp.steps,),
        in_specs=[grp.rows(d), _const((1, d)), sh_spec, sc_spec, _const((d, 2 * d)), _const((1, V_DIM))],
        out_specs=(grp.rows(d),) * 4,
        compiler_params=_params("parallel"),
        name="shared_kv",
    )(h, nw.reshape(1, d), sh, sc, w_bf, knw)


def _s5_operators(a_re, a_im, log_dt, b_re, b_im, c_re, c_im, n_valid):
    f32 = F32
    g, p = a_re.shape
    c = b_re.shape[-1]
    tc = S5_CHUNK
    a_re, a_im = a_re.astype(f32), a_im.astype(f32)
    dt = jnp.exp(log_dt.astype(f32))
    mag = jnp.exp(a_re * dt)
    lb_re, lb_im = mag * jnp.cos(a_im * dt), mag * jnp.sin(a_im * dt)
    den = a_re * a_re + a_im * a_im
    q_re = ((lb_re - 1.0) * a_re + lb_im * a_im) / den
    q_im = (lb_im * a_re - (lb_re - 1.0) * a_im) / den
    bb_re = q_re[..., None] * b_re - q_im[..., None] * b_im
    bb_im = q_re[..., None] * b_im + q_im[..., None] * b_re
    pw_re, pw_im = [jnp.ones_like(lb_re)], [jnp.zeros_like(lb_im)]
    for _ in range(tc):
        r, i = pw_re[-1], pw_im[-1]
        pw_re.append(r * lb_re - i * lb_im)
        pw_im.append(r * lb_im + i * lb_re)
    pw_re, pw_im = jnp.stack(pw_re), jnp.stack(pw_im)
    e_re = pw_re[:tc, :, :, None] * bb_re[None] - pw_im[:tc, :, :, None] * bb_im[None]
    e_im = pw_re[:tc, :, :, None] * bb_im[None] + pw_im[:tc, :, :, None] * bb_re[None]
    kt = (jnp.einsum('gcp,tgpd->tgdc', c_re, e_re, precision=HIGHEST)
          - jnp.einsum('gcp,tgpd->tgdc', c_im, e_im, precision=HIGHEST))
    s_idx = jnp.arange(tc)
    lag = s_idx[None, :] - s_idx[:, None]
    m = jnp.where((lag >= 0)[:, :, None, None, None], kt[jnp.clip(lag, 0, tc - 1)], 0.0)
    m = m.transpose(2, 0, 3, 1, 4).reshape(g // 2, 2, tc * c, tc * c)
    w_re = e_re[::-1].transpose(1, 0, 3, 2).reshape(g // 2, 2, tc * c, p)
    w_im = e_im[::-1].transpose(1, 0, 3, 2).reshape(g // 2, 2, tc * c, p)
    z = jnp.zeros_like(w_re[:, 0])
    w = jnp.concatenate([
        jnp.concatenate([w_re[:, 0], z, w_im[:, 0], z], axis=-1),
        jnp.concatenate([z, w_re[:, 1], z, w_im[:, 1]], axis=-1)], axis=1)
    off = tc - n_valid
    tau = jnp.clip(s_idx - off + 1, 0, tc)
    live = (s_idx >= off)[:, None, None, None]
    d_re = jnp.where(live, c_re[None] * pw_re[tau][:, :, None, :] - c_im[None] * pw_im[tau][:, :, None, :], 0.0)
    d_im = jnp.where(live, c_re[None] * pw_im[tau][:, :, None, :] + c_im[None] * pw_re[tau][:, :, None, :], 0.0)
    cp_re = d_re.transpose(1, 3, 0, 2).reshape(g // 2, 2, p, tc * c)
    cp_im = -d_im.transpose(1, 3, 0, 2).reshape(g // 2, 2, p, tc * c)
    zc = jnp.zeros_like(cp_re[:, 0])
    cp = jnp.stack([
        jnp.concatenate([cp_re[:, 0], zc, cp_im[:, 0], zc], axis=1),
        jnp.concatenate([zc, cp_re[:, 1], zc, cp_im[:, 1]], axis=1)], axis=1)
    are = pw_re[n_valid].reshape(g // 2, 1, 2 * p)
    aim = pw_im[n_valid].reshape(g // 2, 1, 2 * p)
    return w.astype(BF16), m.astype(BF16), cp.astype(BF16), are, aim


def _s5_kernel(u_ref, w_ref, m_ref, cp_ref, are_ref, aim_ref, h0_ref, y_ref, hl_ref, s_scr, hin_scr, *, nk, nb):
    half = s_scr.shape[-1] // 2
    u = u_ref[0]
    s_scr[...] = jnp.dot(u, w_ref[0], preferred_element_type=F32)
    a_re, a_im = are_ref[0], aim_ref[0]
    h0 = h0_ref[0]

    def step(k, carry):
        h_re, h_im = carry
        r0 = k * nb
        if nb % SUBLANES == 0:
            r0 = pl.multiple_of(r0, SUBLANES)
        hin_scr[pl.ds(r0, nb), :half] = h_re
        hin_scr[pl.ds(r0, nb), half:] = h_im
        s = s_scr[pl.ds(r0, nb), :]
        return (a_re * h_re - a_im * h_im + s[:, :half],
                a_re * h_im + a_im * h_re + s[:, half:])

    h_re, h_im = lax.fori_loop(0, nk, step, (h0[:, :half], h0[:, half:]))
    hl_ref[0, :, :half] = h_re
    hl_ref[0, :, half:] = h_im
    hin = hin_scr[...].astype(BF16)
    width = u.shape[-1] // 2
    for gi in range(2):
        sl = slice(gi * width, (gi + 1) * width)
        y = (jnp.dot(u[:, sl], m_ref[0, gi], preferred_element_type=F32)
             + jnp.dot(hin, cp_ref[0, gi], preferred_element_type=F32))
        y_ref[0, :, sl] = y.astype(y_ref.dtype)


def _s5_scan(grp, u, ops, h0_re, h0_im):
    w, m, cp, are, aim = ops
    bt, seq, d = grp.bt, grp.seq, u.shape[-1]
    tc, c = S5_CHUNK, SSM_GROUP_CH
    gp = w.shape[0]
    p = are.shape[-1] // 2
    pad = (-seq) % tc
    u3 = u.reshape(bt, seq, d)
    if pad:
        u3 = jnp.pad(u3, ((0, 0), (pad, 0), (0, 0)))
    nk = (seq + pad) // tc
    rows = nk * bt
    ug = u3.reshape(bt, nk, tc, gp, 2, c).transpose(3, 1, 0, 4, 2, 5).reshape(gp, rows, 2 * tc * c)

    def pair(x):
        return x.astype(F32).reshape(bt, gp, 2 * p).transpose(1, 0, 2)

    h0 = jnp.concatenate([pair(h0_re), pair(h0_im)], axis=-1)
    wide = 2 * tc * c
    y, hl = pl.pallas_call(
        functools.partial(_s5_kernel, nk=nk, nb=bt),
        out_shape=(jax.ShapeDtypeStruct((gp, rows, wide), BF16), jax.ShapeDtypeStruct((gp, bt, 4 * p), F32)),
        grid=(gp,),
        in_specs=[pl.BlockSpec((1, rows, wide), lambda i: (i, 0, 0)),
                  pl.BlockSpec((1, wide, 4 * p), lambda i: (i, 0, 0)),
                  pl.BlockSpec((1, 2, tc * c, tc * c), lambda i: (i, 0, 0, 0)),
                  pl.BlockSpec((1, 2, 4 * p, tc * c), lambda i: (i, 0, 0, 0)),
                  pl.BlockSpec((1, 1, 2 * p), lambda i: (i, 0, 0)),
                  pl.BlockSpec((1, 1, 2 * p), lambda i: (i, 0, 0)),
                  pl.BlockSpec((1, bt, 4 * p), lambda i: (i, 0, 0))],
        out_specs=(pl.BlockSpec((1, rows, wide), lambda i: (i, 0, 0)),
                   pl.BlockSpec((1, bt, 4 * p), lambda i: (i, 0, 0))),
        scratch_shapes=[pltpu.VMEM((rows, 4 * p), F32), pltpu.VMEM((rows, 4 * p), F32)],
        compiler_params=_params("parallel"),
        name="s5_scan",
    )(ug, w, m, cp, are, aim, h0)
    yt = y.reshape(gp, nk, bt, 2, tc, c).transpose(2, 1, 4, 0, 3, 5).reshape(bt, nk * tc, d)
    if pad:
        yt = yt[:, pad:, :]

    def unpair(x):
        return x.transpose(1, 0, 2).reshape(bt, 2 * gp, p)

    return yt.reshape(grp.t, d), unpair(hl[..., :2 * p]), unpair(hl[..., 2 * p:])


def _glu_kernel(y_ref, u_ref, ds_ref, wa_ref, wb_ref, h_ref, g_ref, o_ref):
    z = y_ref[...].astype(F32) + ds_ref[...] * u_ref[...].astype(F32)
    g = jax.nn.gelu(z).astype(BF16)
    a = jnp.dot(g, wa_ref[...], preferred_element_type=F32)
    b = jnp.dot(g, wb_ref[...], preferred_element_type=F32)
    o_ref[...] = h_ref[...] + g_ref[0] * (a * jax.nn.sigmoid(b))


def _glu(grp, y, u, d_skip, wa_bf, wb_bf, h, gate):
    d = h.shape[-1]
    g, g_spec = grp.mod(gate)
    return pl.pallas_call(
        _glu_kernel,
        out_shape=jax.ShapeDtypeStruct((grp.t, d), F32),
        grid=(grp.steps,),
        in_specs=[grp.rows(d), grp.rows(d), _const((1, d)), _const((d, d)), _const((d, d)), grp.rows(d), g_spec],
        out_specs=grp.rows(d),
        compiler_params=_params("parallel"),
        name="s5_glu",
    )(y, u, d_skip.astype(F32).reshape(1, d), wa_bf, wb_bf, h, g)


def _router_kernel(x_ref, nw_ref, sh_ref, sc_ref, wr_ref, br_ref, xf_ref, route_ref):
    xm = _norm_mod(x_ref[...], nw_ref[...], sh_ref[0], sc_ref[0])
    xf_ref[...] = xm
    logits = jnp.dot(xm, wr_ref[...], preferred_element_type=F32, precision=HIGHEST) + br_ref[...]
    lane = lax.broadcasted_iota(jnp.int32, logits.shape, 1).astype(F32)
    big = float(ROUTE_LANES)
    is_g = lane < N_EGROUPS
    gmax = jnp.max(jnp.where(is_g, logits, NEG_INF), axis=-1, keepdims=True)
    gsum = jnp.sum(jnp.where(is_g, jnp.exp(logits - gmax), 0.0), axis=-1, keepdims=True)
    g_sel = jnp.min(jnp.where(is_g & (logits == gmax), lane, big), axis=-1, keepdims=True)
    g_gate = 1.0 / gsum
    lo = N_EGROUPS + EXPERTS_PER_GROUP * g_sel
    is_e = (lane >= lo) & (lane < lo + EXPERTS_PER_GROUP)
    emax = jnp.max(jnp.where(is_e, logits, NEG_INF), axis=-1, keepdims=True)
    ex = jnp.where(is_e, jnp.exp(logits - emax), -1.0)
    p1 = jnp.max(ex, axis=-1, keepdims=True)
    i1 = jnp.min(jnp.where(ex == p1, lane, big), axis=-1, keepdims=True)
    ex2 = jnp.where(lane == i1, -1.0, ex)
    p2 = jnp.max(ex2, axis=-1, keepdims=True)
    i2 = jnp.min(jnp.where(ex2 == p2, lane, big), axis=-1, keepdims=True)
    inv = g_gate / (p1 + p2)
    rec = jnp.where(lane == 0, p1 * inv, 0.0)
    rec = jnp.where(lane == 1, p2 * inv, rec)
    rec = jnp.where(lane == 2, i1 - N_EGROUPS, rec)
    rec = jnp.where(lane == 3, i2 - N_EGROUPS, rec)
    route_ref[...] = rec


def _router(grp, h, nw, shift, scale, wr, br):
    d = h.shape[-1]
    sh, sh_spec = grp.mod(shift)
    sc, sc_spec = grp.mod(scale)
    return pl.pallas_call(
        _router_kernel,
        out_shape=(jax.ShapeDtypeStruct((grp.t, d), F32), jax.ShapeDtypeStruct((grp.t, ROUTE_LANES), F32)),
        grid=(grp.steps,),
        in_specs=[grp.rows(d), _const((1, d)), sh_spec, sc_spec, _const((d, ROUTE_LANES)), _const((1, ROUTE_LANES))],
        out_specs=(grp.rows(d), grp.rows(ROUTE_LANES)),
        compiler_params=_params("parallel"),
        name="moe_router",
    )(h, nw.reshape(1, d), sh, sc, wr, br)


def _moe_plan(route, bm):
    t = route.shape[0]
    a = 2 * t
    blk = math.gcd(a, LANES)
    e = route[:, 2:4].astype(jnp.int32).reshape(a)
    oh = e[:, None] == jnp.arange(N_EXPERTS, dtype=jnp.int32)[None, :]
    tri = jnp.tril(jnp.ones((blk, blk), BF16))
    within = jnp.einsum('ij,bjk->bik', tri, oh.astype(BF16).reshape(a // blk, blk, N_EXPERTS),
                        preferred_element_type=F32)
    tot = within[:, -1, :]
    offs = jnp.cumsum(tot, axis=0) - tot
    rank = (within + offs[:, None, :]).reshape(a, N_EXPERTS)
    counts = (offs[-1] + tot[-1]).astype(jnp.int32)
    padded = (counts + bm - 1) // bm * bm
    pad_end = jnp.cumsum(padded)
    pad_start = pad_end - padded
    ohf = oh.astype(F32)
    dest = jnp.sum(ohf * (rank - 1.0 + pad_start.astype(F32)[None, :]), axis=-1).astype(jnp.int32)
    n_blocks = a // bm + N_EXPERTS
    first_row = jnp.arange(n_blocks, dtype=jnp.int32) * bm
    block_expert = jnp.minimum(jnp.searchsorted(pad_end, first_row, side='right'), N_EXPERTS - 1).astype(jnp.int32)
    block_live = (first_row < pad_end[-1]).astype(jnp.int32)
    return dest, block_expert, block_live, n_blocks


def _dispatch_kernel(dest_ref, x_ref, xb_in, xb_out, sem):
    del xb_in
    tm = x_ref.shape[0]

    def copy(r, k):
        dst = dest_ref[0, 0, 2 * r + k]
        return pltpu.make_async_copy(x_ref.at[pl.ds(r, 1)], xb_out.at[pl.ds(dst, 1)], sem)

    def issue(r, carry):
        copy(r, 0).start()
        copy(r, 1).start()
        return carry

    def drain(r, carry):
        copy(r, 0).wait()
        copy(r, 1).wait()
        return carry

    lax.fori_loop(0, tm, issue, 0)
    lax.fori_loop(0, tm, drain, 0)


def _dispatch(grp, xf, dest, n_rows):
    d = xf.shape[-1]
    tm = min(grp.tm, 256)
    steps = grp.t // tm
    return pl.pallas_call(
        _dispatch_kernel,
        out_shape=jax.ShapeDtypeStruct((n_rows, d), F32),
        grid=(steps,),
        in_specs=[pl.BlockSpec((1, 1, 2 * tm), lambda i: (i, 0, 0), memory_space=pltpu.SMEM),
                  pl.BlockSpec((tm, d), lambda i: (i, 0)),
                  pl.BlockSpec(memory_space=pl.ANY)],
        out_specs=pl.BlockSpec(memory_space=pl.ANY),
        scratch_shapes=[pltpu.SemaphoreType.DMA],
        input_output_aliases={2: 0},
        compiler_params=_params("arbitrary"),
        name="moe_dispatch",
    )(dest.reshape(steps, 1, 2 * tm), xf, jnp.zeros((n_rows, d), F32))


def _expert_kernel(be_ref, live_ref, x_ref, w1_ref, w3_ref, w2_ref, o_ref):
    del be_ref
    i = pl.program_id(0)

    @pl.when(live_ref[i] > 0)
    def _():
        x = x_ref[...].astype(BF16)
        a = jnp.dot(x, w1_ref[0], preferred_element_type=F32)
        b = jnp.dot(x, w3_ref[0], preferred_element_type=F32)
        hid = (a * jax.nn.sigmoid(a) * b).astype(BF16)
        o_ref[...] = jnp.dot(hid, w2_ref[0], preferred_element_type=F32)

    @pl.when(live_ref[i] == 0)
    def _():
        o_ref[...] = jnp.zeros_like(o_ref)


def _experts(xb, block_expert, block_live, bm, w1_bf, w3_bf, w2_bf):
    n_rows, d = xb.shape
    f = w1_bf.shape[-1]
    return pl.pallas_call(
        _expert_kernel,
        out_shape=jax.ShapeDtypeStruct((n_rows, d), F32),
        grid_spec=pltpu.PrefetchScalarGridSpec(
            num_scalar_prefetch=2,
            grid=(n_rows // bm,),
            in_specs=[pl.BlockSpec((bm, d), lambda i, be, lv: (i, 0)),
                      pl.BlockSpec((1, d, f), lambda i, be, lv: (be[i], 0, 0)),
                      pl.BlockSpec((1, d, f), lambda i, be, lv: (be[i], 0, 0)),
                      pl.BlockSpec((1, f, d), lambda i, be, lv: (be[i], 0, 0))],
            out_specs=pl.BlockSpec((bm, d), lambda i, be, lv: (i, 0))),
        compiler_params=_params("arbitrary"),
        name="moe_experts",
    )(block_expert, block_live, xb, w1_bf, w3_bf, w2_bf)


def _combine_kernel(dest_ref, h_ref, g_ref, route_ref, yb_hbm, o_ref, gbuf, sem):
    tm = h_ref.shape[0]

    def copy(r, k):
        src = dest_ref[0, 0, 2 * r + k]
        return pltpu.make_async_copy(yb_hbm.at[pl.ds(src, 1)], gbuf.at[k, pl.ds(r, 1)], sem)

    def issue(r, carry):
        copy(r, 0).start()
        copy(r, 1).start()
        return carry

    def drain(r, carry):
        copy(r, 0).wait()
        copy(r, 1).wait()
        return carry

    lax.fori_loop(0, tm, issue, 0)
    lax.fori_loop(0, tm, drain, 0)
    rec = route_ref[...]
    o_ref[...] = h_ref[...] + g_ref[0] * (rec[:, 0:1] * gbuf[0] + rec[:, 1:2] * gbuf[1])


def _combine(grp, h, gate, route, dest, yb):
    d = h.shape[-1]
    tm = min(grp.tm, 256)
    steps = grp.t // tm
    if grp.per_batch:
        per = grp.seq // tm
        g, g_spec = gate[:, None, :], pl.BlockSpec((1, 1, d), lambda i: (i // per, 0, 0))
    else:
        g, g_spec = grp.mod(gate)
    return pl.pallas_call(
        _combine_kernel,
        out_shape=jax.ShapeDtypeStruct((grp.t, d), F32),
        grid=(steps,),
        in_specs=[pl.BlockSpec((1, 1, 2 * tm), lambda i: (i, 0, 0), memory_space=pltpu.SMEM),
                  pl.BlockSpec((tm, d), lambda i: (i, 0)),
                  g_spec,
                  pl.BlockSpec((tm, ROUTE_LANES), lambda i: (i, 0)),
                  pl.BlockSpec(memory_space=pl.ANY)],
        out_specs=pl.BlockSpec((tm, d), lambda i: (i, 0)),
        scratch_shapes=[pltpu.VMEM((2, tm, d), F32), pltpu.SemaphoreType.DMA],
        compiler_params=_params("arbitrary"),
        name="moe_combine",
    )(dest.reshape(steps, 1, 2 * tm), h, g, route, yb)


def _moe(grp, h, nw, shift, scale, gate, wr, br, w1_bf, w3_bf, w2_bf):
    bm = 256 if grp.t >= 2048 else 32
    xf, route = _router(grp, h, nw, shift, scale, wr, br)
    dest, block_expert, block_live, n_blocks = _moe_plan(route, bm)
    xb = _dispatch(grp, xf, dest, n_blocks * bm)
    yb = _experts(xb, block_expert, block_live, bm, w1_bf, w3_bf, w2_bf)
    return _combine(grp, h, gate, route, dest, yb)


def _attn_kernel(lam_ref, q_ref, k_ref, v_ref, qnw_ref, sub_ref, o_ref, m_scr, l_scr, acc_scr, *, out_scale):
    tq = q_ref.shape[1]
    tk = tq
    qi = pl.program_id(2)
    qn = _half_norm(q_ref[0].astype(F32), qnw_ref[...], HEAD_DIM ** -0.5)
    lane = lax.broadcasted_iota(jnp.int32, qn.shape, 1)
    low = lane < HEAD_DIM
    qz = jnp.concatenate([jnp.where(low, qn, 0.0), jnp.where(low, 0.0, qn)], axis=0).astype(BF16)
    m_scr[...] = jnp.full_like(m_scr, NEG_INF)
    l_scr[...] = jnp.zeros_like(l_scr)
    acc_scr[...] = jnp.zeros_like(acc_scr)

    def tile(j, masked):
        r0 = pl.multiple_of(j * tk, tk)
        k = k_ref[0, pl.ds(r0, tk), :]
        v = v_ref[0, pl.ds(r0, tk), :]
        s = lax.dot_general(qz, k, (((1,), (1,)), ((), ())), preferred_element_type=F32)
        if masked:
            row = lax.broadcasted_iota(jnp.int32, (2 * tq, tk), 0)
            col = lax.broadcasted_iota(jnp.int32, (2 * tq, tk), 1)
            s = jnp.where(col <= jnp.where(row >= tq, row - tq, row), s, NEG_INF)
        m_old = m_scr[...]
        m_new = jnp.maximum(m_old, jnp.max(s, axis=-1, keepdims=True))
        corr = jnp.exp(m_old - m_new)
        p = jnp.exp(s - m_new)
        l_scr[...] = l_scr[...] * corr + jnp.sum(p, axis=-1, keepdims=True)
        acc_scr[...] = acc_scr[...] * corr + jnp.dot(p.astype(BF16), v, preferred_element_type=F32)
        m_scr[...] = m_new

    def full(j, carry):
        tile(j, False)
        return carry

    lax.fori_loop(0, qi, full, 0)
    tile(qi, True)
    acc = acc_scr[...] / l_scr[...]
    o = acc[:tq] - lam_ref[0] * acc[tq:]
    o = o * lax.rsqrt(jnp.mean(o * o, axis=-1, keepdims=True) + RMS_EPS) * sub_ref[...] * out_scale
    o_ref[0] = o.astype(o_ref.dtype)


def _attn_prompt(grp, q, kb, vb, lam, qnw, subw, out_scale):
    bt, seq, d = grp.bt, grp.seq, q.shape[-1]
    nh = d // V_DIM
    tq = min(ATTN_TILE, seq)
    nq = seq // tq
    out = pl.pallas_call(
        functools.partial(_attn_kernel, out_scale=out_scale),
        out_shape=jax.ShapeDtypeStruct((bt, seq, d), BF16),
        grid_spec=pltpu.PrefetchScalarGridSpec(
            num_scalar_prefetch=1,
            grid=(bt, nh, nq),
            in_specs=[pl.BlockSpec((1, tq, V_DIM), lambda b, h, i, lam: (b, i, h)),
                      pl.BlockSpec((1, seq, V_DIM), lambda b, h, i, lam: (b, 0, h)),
                      pl.BlockSpec((1, seq, V_DIM), lambda b, h, i, lam: (b, 0, h)),
                      pl.BlockSpec((1, V_DIM), lambda b, h, i, lam: (0, 0)),
                      pl.BlockSpec((1, V_DIM), lambda b, h, i, lam: (0, 0))],
            out_specs=pl.BlockSpec((1, tq, V_DIM), lambda b, h, i, lam: (b, i, h)),
            scratch_shapes=[pltpu.VMEM((2 * tq, 1), F32), pltpu.VMEM((2 * tq, 1), F32),
                            pltpu.VMEM((2 * tq, V_DIM), F32)]),
        compiler_params=_params("parallel", "parallel", "arbitrary"),
        name="diff_attn_prompt",
    )(lam.reshape(1), q.reshape(bt, seq, d), kb.reshape(bt, seq, d), vb.reshape(bt, seq, d), qnw, subw)
    return out.reshape(grp.t, d)


def _paged_kernel(pt_ref, lam_ref, q_ref, kn_ref, vn_ref, qnw_ref, sub_ref, ck_hbm, cv_hbm, o_ref,
                  kbuf, vbuf, sem, qz_scr, m_scr, l_scr, acc_scr, *, n_chunks, out_scale):
    b = pl.program_id(0)
    nb = pl.num_programs(0)
    pps = kbuf.shape[1]
    page = kbuf.shape[2] // (q_ref.shape[-1] // V_DIM)
    t = q_ref.shape[1]
    nh = q_ref.shape[-1] // V_DIM

    def copies(seq_i, chunk, slot):
        out = []
        for pg in range(pps):
            phys = pt_ref[seq_i, chunk * pps + pg]
            out.append(pltpu.make_async_copy(ck_hbm.at[phys], kbuf.at[slot, pg], sem.at[slot, 0]))
            out.append(pltpu.make_async_copy(cv_hbm.at[phys], vbuf.at[slot, pg], sem.at[slot, 1]))
        return out

    @pl.when(b == 0)
    def _():
        for cp in copies(0, 0, 0):
            cp.start()

    for hd in range(nh):
        sl = slice(hd * V_DIM, (hd + 1) * V_DIM)
        qn = _half_norm(q_ref[0, :, sl].astype(F32), qnw_ref[...], HEAD_DIM ** -0.5)
        lane = lax.broadcasted_iota(jnp.int32, qn.shape, 1)
        low = lane < HEAD_DIM
        qz_scr[hd] = jnp.concatenate([jnp.where(low, qn, 0.0), jnp.where(low, 0.0, qn)], axis=0)
    m_scr[...] = jnp.full_like(m_scr, NEG_INF)
    l_scr[...] = jnp.zeros_like(l_scr)
    acc_scr[...] = jnp.zeros_like(acc_scr)

    def update(hd, s, pv):
        m_old = m_scr[hd]
        m_new = jnp.maximum(m_old, jnp.max(s, axis=-1, keepdims=True))
        corr = jnp.exp(m_old - m_new)
        p = jnp.exp(s - m_new)
        l_scr[hd] = l_scr[hd] * corr + jnp.sum(p, axis=-1, keepdims=True)
        acc_scr[hd] = acc_scr[hd] * corr + pv(p)
        m_scr[hd] = m_new

    def chunk_step(c, carry):
        it = b * n_chunks + c
        slot = lax.rem(it, 2)
        nxt = it + 1
        nseq = nxt // n_chunks

        @pl.when(nseq < nb)
        def _():
            for cp in copies(nseq, nxt - nseq * n_chunks, 1 - slot):
                cp.start()

        for cp in copies(b, c, slot):
            cp.wait()
        for hd in range(nh):
            kh = jnp.concatenate([kbuf[slot, pg, pl.ds(hd, page, stride=nh), :] for pg in range(pps)], axis=0)
            vh = jnp.concatenate([vbuf[slot, pg, pl.ds(hd, page, stride=nh), :] for pg in range(pps)], axis=0)
            s = lax.dot_general(qz_scr[hd].astype(BF16), kh.astype(BF16), (((1,), (1,)), ((), ())),
                                preferred_element_type=F32)
            vhb = vh.astype(BF16)
            update(hd, s, lambda p: jnp.dot(p.astype(BF16), vhb, preferred_element_type=F32))
        return carry

    lax.fori_loop(0, n_chunks, chunk_step, 0)

    row = lax.broadcasted_iota(jnp.int32, (2 * t, t), 0)
    col = lax.broadcasted_iota(jnp.int32, (2 * t, t), 1)
    causal = col <= jnp.where(row >= t, row - t, row)
    for hd in range(nh):
        sl = slice(hd * V_DIM, (hd + 1) * V_DIM)
        kn = kn_ref[0, :, sl]
        vn = vn_ref[0, :, sl]
        qz = qz_scr[hd]
        s = jnp.concatenate([jnp.sum(qz * kn[j:j + 1, :], axis=-1, keepdims=True) for j in range(t)], axis=-1)
        s = jnp.where(causal, s, NEG_INF)

        def pv_new(p, vn=vn):
            out = p[:, 0:1] * vn[0:1, :]
            for j in range(1, t):
                out = out + p[:, j:j + 1] * vn[j:j + 1, :]
            return out

        update(hd, s, pv_new)
        acc = acc_scr[hd] / l_scr[hd]
        o = acc[:t] - lam_ref[0] * acc[t:]
        o = o * lax.rsqrt(jnp.mean(o * o, axis=-1, keepdims=True) + RMS_EPS) * sub_ref[...] * out_scale
        o_ref[0, :, sl] = o.astype(o_ref.dtype)


def _attn_sample(grp, q, k_new, v_new, cache_k, cache_v, page_table, lam, qnw, subw, out_scale):
    bd, t, d = grp.bt, grp.seq, q.shape[-1]
    n_phys, page, nh, vd = cache_k.shape
    n_pages = page_table.shape[1]
    pps = math.gcd(PAGES_PER_STEP, n_pages)
    n_chunks = n_pages // pps
    ck = cache_k.reshape(n_phys, page * nh, vd)
    cv = cache_v.reshape(n_phys, page * nh, vd)
    seq_spec = pl.BlockSpec((1, t, d), lambda b, pt, lam: (b, 0, 0))
    out = pl.pallas_call(
        functools.partial(_paged_kernel, n_chunks=n_chunks, out_scale=out_scale),
        out_shape=jax.ShapeDtypeStruct((bd, t, d), BF16),
        grid_spec=pltpu.PrefetchScalarGridSpec(
            num_scalar_prefetch=2,
            grid=(bd,),
            in_specs=[seq_spec, seq_spec, seq_spec,
                      pl.BlockSpec((1, vd), lambda b, pt, lam: (0, 0)),
                      pl.BlockSpec((1, vd), lambda b, pt, lam: (0, 0)),
                      pl.BlockSpec(memory_space=pl.ANY),
                      pl.BlockSpec(memory_space=pl.ANY)],
            out_specs=seq_spec,
            scratch_shapes=[pltpu.VMEM((2, pps, page * nh, vd), F32),
                            pltpu.VMEM((2, pps, page * nh, vd), F32),
                            pltpu.SemaphoreType.DMA((2, 2)),
                            pltpu.VMEM((nh, 2 * t, vd), F32),
                            pltpu.VMEM((nh, 2 * t, 1), F32),
                            pltpu.VMEM((nh, 2 * t, 1), F32),
                            pltpu.VMEM((nh, 2 * t, vd), F32)]),
        compiler_params=_params("arbitrary"),
        name="diff_attn_paged",
    )(page_table.astype(jnp.int32), lam.reshape(1), q.reshape(bd, t, d), k_new.reshape(bd, t, d),
      v_new.reshape(bd, t, d), qnw, subw, ck, cv)
    return out.reshape(grp.t, d)


def _out_proj_kernel(o_ref, w_ref, h_ref, g_ref, y_ref):
    y_ref[...] = h_ref[...] + g_ref[0] * jnp.dot(o_ref[...], w_ref[...], preferred_element_type=F32)


def _out_proj(grp, o, w_bf, h, gate):
    d = h.shape[-1]
    g, g_spec = grp.mod(gate)
    return pl.pallas_call(
        _out_proj_kernel,
        out_shape=jax.ShapeDtypeStruct((grp.t, d), F32),
        grid=(grp.steps,),
        in_specs=[grp.rows(d), _const((d, d)), grp.rows(d), g_spec],
        out_specs=grp.rows(d),
        compiler_params=_params("parallel"),
        name="attn_out_proj",
    )(o, w_bf, h, g)


def _trunk(x, mods, mod_kv, h0_re, h0_im, kv_past, wts):
    bt, seq, d = x.shape
    grp = _Group(bt, seq)
    h = x.astype(F32).reshape(grp.t, d)
    depth = len(mods)
    n_a = depth // 2
    new_re, new_im = [], []
    k = v = kb = vb = None
    for l in range(depth):
        if l == n_a:
            k, v, kb, vb = _kv_proj(grp, h, wts['norm_kv_w'], mod_kv[:, :d], mod_kv[:, d:], wts['w_kv'],
                                    wts['k_norm_w'])
        sh1, sc1, g1, sh2, sc2, g2 = jnp.split(mods[l], 6, axis=-1)
        if l < n_a:
            u = _norm_linear(grp, h, wts['norm_mix_w'][l], sh1, sc1, wts['ssm_w_in'][l], "s5_in_proj")
            n_valid = S5_CHUNK - (-seq) % S5_CHUNK if seq < S5_CHUNK else S5_CHUNK
            ops = _s5_operators(wts['ssm_a_re'][l], wts['ssm_a_im'][l], wts['ssm_log_dt'][l], wts['ssm_b_re'][l],
                                wts['ssm_b_im'][l], wts['ssm_c_re'][l], wts['ssm_c_im'][l], n_valid)
            y, hl_re, hl_im = _s5_scan(grp, u, ops, h0_re[l], h0_im[l])
            new_re.append(hl_re)
            new_im.append(hl_im)
            h = _glu(grp, y, u, wts['ssm_d_skip'][l], wts['ssm_w_glu_a'][l], wts['ssm_w_glu_b'][l], h, g1)
        else:
            j = l - n_a
            q = _norm_linear(grp, h, wts['norm_mix_w'][l], sh1, sc1, wts['attn_w_q'][j], "attn_q_proj")
            lam_init = 0.8 - 0.6 * math.exp(-0.3 * l)
            f32 = F32
            lam = (jnp.exp(jnp.sum(wts['lambda_q1'][j].astype(f32) * wts['lambda_k1'][j].astype(f32)))
                   - jnp.exp(jnp.sum(wts['lambda_q2'][j].astype(f32) * wts['lambda_k2'][j].astype(f32))) + lam_init)
            qnw = jnp.tile(wts['q_norm_w'][j].astype(f32), V_DIM // HEAD_DIM).reshape(1, V_DIM)
            subw = wts['subln_w'][j].astype(f32).reshape(1, V_DIM)
            if kv_past is None:
                o = _attn_prompt(grp, q, kb, vb, lam, qnw, subw, 1.0 - lam_init)
            else:
                o = _attn_sample(grp, q, k, v, kv_past[0], kv_past[1], kv_past[2], lam, qnw, subw, 1.0 - lam_init)
            h = _out_proj(grp, o, wts['attn_w_o'][j], h, g1)
        h = _moe(grp, h, wts['norm_ffn_w'][l], sh2, sc2, g2, wts['moe_wr'][l], wts['moe_br'][l],
                 wts['moe_w1'][l], wts['moe_w3'][l], wts['moe_w2'][l])
    nh = d // V_DIM
    return (h.reshape(bt, seq, d), k.reshape(bt, seq, nh, V_DIM), v.reshape(bt, seq, nh, V_DIM),
            jnp.stack(new_re), jnp.stack(new_im))


def kernel(x_prompt, x_sample, cache_k, cache_v, page_table, state_ssm_re, state_ssm_im, c_prompt, c_sample,
           w_mod, b_mod, norm_mix_w, norm_ffn_w,
           ssm_w_in, ssm_a_re, ssm_a_im, ssm_log_dt, ssm_b_re, ssm_b_im, ssm_c_re, ssm_c_im, ssm_d_skip, ssm_w_glu,
           w_mod_kv, b_mod_kv, norm_kv_w, w_kv, k_norm_w,
           attn_w_q, q_norm_w, lambda_q1, lambda_k1, lambda_q2, lambda_k2, subln_w, attn_w_o,
           moe_w_group, moe_b_group, moe_w_erouter, moe_b_erouter, moe_w1, moe_w3, moe_w2):
    d = x_prompt.shape[-1]
    depth = w_mod.shape[0]
    n_prompt = c_prompt.shape[0]
    c_all = jnp.concatenate([c_prompt, c_sample], axis=0).astype(F32)
    mods = [_mod_linear(c_all, w_mod[l], b_mod[l]) for l in range(depth)]
    mod_kv = _mod_linear(c_all, w_mod_kv, b_mod_kv)
    pad = ROUTE_LANES - N_EGROUPS - N_EXPERTS
    wr = jnp.concatenate([moe_w_group, moe_w_erouter.transpose(0, 2, 1, 3).reshape(depth, d, N_EXPERTS),
                          jnp.zeros((depth, d, pad), F32)], axis=-1)
    br = jnp.concatenate([moe_b_group, moe_b_erouter.reshape(depth, N_EXPERTS),
                          jnp.zeros((depth, pad), F32)], axis=-1).reshape(depth, 1, ROUTE_LANES)
    wts = dict(norm_mix_w=norm_mix_w, norm_ffn_w=norm_ffn_w, norm_kv_w=norm_kv_w, k_norm_w=k_norm_w,
               ssm_w_in=ssm_w_in.astype(BF16), ssm_a_re=ssm_a_re, ssm_a_im=ssm_a_im, ssm_log_dt=ssm_log_dt,
               ssm_b_re=ssm_b_re, ssm_b_im=ssm_b_im, ssm_c_re=ssm_c_re, ssm_c_im=ssm_c_im, ssm_d_skip=ssm_d_skip,
               ssm_w_glu_a=ssm_w_glu[..., :d].astype(BF16), ssm_w_glu_b=ssm_w_glu[..., d:].astype(BF16),
               w_kv=w_kv.astype(BF16), attn_w_q=attn_w_q.astype(BF16), q_norm_w=q_norm_w,
               lambda_q1=lambda_q1, lambda_k1=lambda_k1, lambda_q2=lambda_q2, lambda_k2=lambda_k2,
               subln_w=subln_w, attn_w_o=attn_w_o.astype(BF16), moe_wr=wr, moe_br=br,
               moe_w1=moe_w1.astype(BF16), moe_w3=moe_w3.astype(BF16), moe_w2=moe_w2.astype(BF16))
    zero = jnp.zeros((state_ssm_re.shape[0], n_prompt) + state_ssm_re.shape[2:], F32)
    y_p, k_p, v_p, re_p, im_p = _trunk(x_prompt, [m[:n_prompt] for m in mods], mod_kv[:n_prompt],
                                       zero, zero, None, wts)
    y_s, k_s, v_s, re_s, im_s = _trunk(x_sample, [m[n_prompt:] for m in mods], mod_kv[n_prompt:],
                                       state_ssm_re, state_ssm_im, (cache_k, cache_v, page_table), wts)
    return (y_p, y_s, k_p, v_p, k_s, v_s, re_p, im_p, re_s, im_s)
```

```python
import functools
import math

import jax
import jax.numpy as jnp
from jax import lax
from jax.experimental import pallas as pl
from jax.experimental.pallas import tpu as pltpu

F32 = jnp.float32
BF16 = jnp.bfloat16
HIGHEST = lax.Precision.HIGHEST

SSM_GROUP_CH = 16
HEAD_DIM = 64
V_DIM = 2 * HEAD_DIM
N_EGROUPS = 4
EXPERTS_PER_GROUP = 8
N_EXPERTS = N_EGROUPS * EXPERTS_PER_GROUP
RMS_EPS = 1e-6
NEG_INF = -1e30

LANES = 128
SUBLANES = 8
MXU_DIM = 256

S5_CHUNK = MXU_DIM // SSM_GROUP_CH
S5_SEQS_PER_STEP = 4
ROW_TILE = 512
ATTN_TILE = 512
ATTN_ROWS = 256
LOG2E = 1.4426950408889634
FIXED_SHIFT_MAX = 60.0
ROW_UNROLL = 8
PAGES_PER_STEP = 4
ROUTE_LANES = LANES
VMEM_LIMIT = 48 * 1024 * 1024


def _params(*sem):
    return pltpu.CompilerParams(dimension_semantics=sem, vmem_limit_bytes=VMEM_LIMIT)


def _norm_mod(x, nw, shift, scale):
    ms = jnp.mean(x * x, axis=-1, keepdims=True)
    return x * lax.rsqrt(ms + RMS_EPS) * nw * (1.0 + scale) + shift


def _half_norm(x, w, scale):
    lane = lax.broadcasted_iota(jnp.int32, x.shape, x.ndim - 1)
    low = lane < HEAD_DIM
    sq = x * x
    tot = jnp.sum(sq, axis=-1, keepdims=True)
    lo = jnp.sum(jnp.where(low, sq, 0.0), axis=-1, keepdims=True)
    ms = jnp.where(low, lo, tot - lo) * (1.0 / HEAD_DIM)
    return x * lax.rsqrt(ms + RMS_EPS) * w * scale


class _Group:
    def __init__(self, bt, seq):
        self.bt, self.seq, self.t = bt, seq, bt * seq
        if seq % ROW_TILE == 0:
            self.tm, self.per_batch = ROW_TILE, True
        else:
            assert self.t <= ROW_TILE and self.t % SUBLANES == 0
            self.tm, self.per_batch = self.t, False
        self.steps = self.t // self.tm

    def mod(self, vec):
        d = vec.shape[-1]
        if self.per_batch:
            per = self.seq // self.tm
            return vec[:, None, :], pl.BlockSpec((1, 1, d), lambda i: (i // per, 0, 0))
        arr = jnp.repeat(vec, self.seq, axis=0).reshape(self.steps, self.tm, d)
        return arr, pl.BlockSpec((1, self.tm, d), lambda i: (i, 0, 0))

    def rows(self, d):
        return pl.BlockSpec((self.tm, d), lambda i: (i, 0))


def _const(shape):
    return pl.BlockSpec(shape, lambda i: (0,) * len(shape))


def _mod_kernel(c_ref, w_ref, b_ref, o_ref):
    c = c_ref[...]
    a = c * jax.nn.sigmoid(c)
    o_ref[...] = jnp.dot(a, w_ref[...], preferred_element_type=F32, precision=HIGHEST) + b_ref[...]


def _mod_linear(c, w, b):
    m, d = c.shape
    n = w.shape[1]
    tn = 1024
    return pl.pallas_call(
        _mod_kernel,
        out_shape=jax.ShapeDtypeStruct((m, n), F32),
        grid=(n // tn,),
        in_specs=[pl.BlockSpec((m, d), lambda j: (0, 0)),
                  pl.BlockSpec((d, tn), lambda j: (0, j)),
                  pl.BlockSpec((1, tn), lambda j: (0, j))],
        out_specs=pl.BlockSpec((m, tn), lambda j: (0, j)),
        compiler_params=_params("parallel"),
        name="adaln_mod",
    )(c, w, b.reshape(1, n))


def _norm_linear_kernel(x_ref, nw_ref, sh_ref, sc_ref, w_ref, o_ref):
    xm = _norm_mod(x_ref[...], nw_ref[...], sh_ref[0], sc_ref[0])
    o_ref[...] = jnp.dot(xm.astype(BF16), w_ref[...], preferred_element_type=F32).astype(o_ref.dtype)


def _norm_linear(grp, h, nw, shift, scale, w_bf, name):
    d, n = w_bf.shape
    sh, sh_spec = grp.mod(shift)
    sc, sc_spec = grp.mod(scale)
    return pl.pallas_call(
        _norm_linear_kernel,
        out_shape=jax.ShapeDtypeStruct((grp.t, n), BF16),
        grid=(grp.steps,),
        in_specs=[grp.rows(d), _const((1, d)), sh_spec, sc_spec, _const((d, n))],
        out_specs=grp.rows(n),
        compiler_params=_params("parallel"),
        name=name,
    )(h, nw.reshape(1, d), sh, sc, w_bf)


def _kv_kernel(x_ref, nw_ref, sh_ref, sc_ref, w_ref, knw_ref, k_ref, v_ref, kb_ref, vb_ref):
    d = x_ref.shape[-1]
    xm = _norm_mod(x_ref[...], nw_ref[...], sh_ref[0], sc_ref[0])
    kv = jnp.dot(xm.astype(BF16), w_ref[...], preferred_element_type=F32)
    v = kv[:, d:]
    v_ref[...] = v
    vb_ref[...] = v.astype(BF16)
    for hd in range(d // V_DIM):
        sl = slice(hd * V_DIM, (hd + 1) * V_DIM)
        kn = _half_norm(kv[:, sl], knw_ref[...], 1.0)
        k_ref[:, sl] = kn
        kb_ref[:, sl] = kn.astype(BF16)


def _kv_proj(grp, h, nw, shift, scale, w_bf, k_norm_w):
    d = h.shape[-1]
    sh, sh_spec = grp.mod(shift)
    sc, sc_spec = grp.mod(scale)
    knw = jnp.tile(k_norm_w.astype(F32), V_DIM // HEAD_DIM).reshape(1, V_DIM)
    out = jax.ShapeDtypeStruct((grp.t, d), F32)
    outb = jax.ShapeDtypeStruct((grp.t, d), BF16)
    return pl.pallas_call(
        _kv_kernel,
        out_shape=(out, out, outb, outb),
        grid=(grp.steps,),
        in_specs=[grp.rows(d), _const((1, d)), sh_spec, sc_spec, _const((d, 2 * d)), _const((1, V_DIM))],
        out_specs=(grp.rows(d),) * 4,
        compiler_params=_params("parallel"),
        name="shared_kv",
    )(h, nw.reshape(1, d), sh, sc, w_bf, knw)


def _s5_operators(a_re, a_im, log_dt, b_re, b_im, c_re, c_im, n_valid):
    f32 = F32
    g, p = a_re.shape
    c = b_re.shape[-1]
    tc = S5_CHUNK
    a_re, a_im = a_re.astype(f32), a_im.astype(f32)
    dt = jnp.exp(log_dt.astype(f32))
    mag = jnp.exp(a_re * dt)
    lb_re, lb_im = mag * jnp.cos(a_im * dt), mag * jnp.sin(a_im * dt)
    den = a_re * a_re + a_im * a_im
    q_re = ((lb_re - 1.0) * a_re + lb_im * a_im) / den
    q_im = (lb_im * a_re - (lb_re - 1.0) * a_im) / den
    bb_re = q_re[..., None] * b_re - q_im[..., None] * b_im
    bb_im = q_re[..., None] * b_im + q_im[..., None] * b_re
    pw_re, pw_im = [jnp.ones_like(lb_re)], [jnp.zeros_like(lb_im)]
    for _ in range(tc):
        r, i = pw_re[-1], pw_im[-1]
        pw_re.append(r * lb_re - i * lb_im)
        pw_im.append(r * lb_im + i * lb_re)
    pw_re, pw_im = jnp.stack(pw_re), jnp.stack(pw_im)
    e_re = pw_re[:tc, :, :, None] * bb_re[None] - pw_im[:tc, :, :, None] * bb_im[None]
    e_im = pw_re[:tc, :, :, None] * bb_im[None] + pw_im[:tc, :, :, None] * bb_re[None]
    kt = (jnp.einsum('gcp,tgpd->tgdc', c_re, e_re, precision=HIGHEST)
          - jnp.einsum('gcp,tgpd->tgdc', c_im, e_im, precision=HIGHEST))
    s_idx = jnp.arange(tc)
    lag = s_idx[None, :] - s_idx[:, None]
    m = jnp.where((lag >= 0)[:, :, None, None, None], kt[jnp.clip(lag, 0, tc - 1)], 0.0)
    m = m.transpose(2, 0, 3, 1, 4).reshape(g // 2, 2, tc * c, tc * c)
    w_re = e_re[::-1].transpose(1, 0, 3, 2).reshape(g // 2, 2, tc * c, p)
    w_im = e_im[::-1].transpose(1, 0, 3, 2).reshape(g // 2, 2, tc * c, p)
    z = jnp.zeros_like(w_re[:, 0])
    w = jnp.concatenate([
        jnp.concatenate([w_re[:, 0], z, w_im[:, 0], z], axis=-1),
        jnp.concatenate([z, w_re[:, 1], z, w_im[:, 1]], axis=-1)], axis=1)
    off = tc - n_valid
    tau = jnp.clip(s_idx - off + 1, 0, tc)
    live = (s_idx >= off)[:, None, None, None]
    d_re = jnp.where(live, c_re[None] * pw_re[tau][:, :, None, :] - c_im[None] * pw_im[tau][:, :, None, :], 0.0)
    d_im = jnp.where(live, c_re[None] * pw_im[tau][:, :, None, :] + c_im[None] * pw_re[tau][:, :, None, :], 0.0)
    cp_re = d_re.transpose(1, 3, 0, 2).reshape(g // 2, 2, p, tc * c)
    cp_im = -d_im.transpose(1, 3, 0, 2).reshape(g // 2, 2, p, tc * c)
    zc = jnp.zeros_like(cp_re[:, 0])
    cp = jnp.stack([
        jnp.concatenate([cp_re[:, 0], zc, cp_im[:, 0], zc], axis=1),
        jnp.concatenate([zc, cp_re[:, 1], zc, cp_im[:, 1]], axis=1)], axis=1)
    are = pw_re[n_valid].reshape(g // 2, 1, 2 * p)
    aim = pw_im[n_valid].reshape(g // 2, 1, 2 * p)
    return w.astype(BF16), m.astype(BF16), cp.astype(BF16), are, aim


def _move_lanes(x, src, dst):
    shift = (dst - src) % LANES
    return pltpu.roll(x, shift, axis=1) if shift else x


def _s5_kernel(u_ref, w_ref, m_ref, cp_ref, are_ref, aim_ref, h0_ref, y_ref, hl_ref,
               u_scr, sre_scr, sim_scr, hre_scr, him_scr, y_scr, *, nk, nb):
    tc = u_ref.shape[0]
    c = SSM_GROUP_CH
    per_tile = LANES // c
    half = sre_scr.shape[-1]
    width = tc * c
    n_pairs = w_ref.shape[0]
    lane = lax.broadcasted_iota(jnp.int32, (u_ref.shape[1] // 2, LANES), 1)

    for pi in range(n_pairs):
        for gi in range(2):
            for j in range(tc // per_tile):
                tile = jnp.zeros(lane.shape, jnp.int32)
                for tt in range(per_tile):
                    src = pltpu.bitcast(u_ref[j * per_tile + tt], jnp.int32)
                    moved = _move_lanes(src, (2 * pi + gi) * c, tt * c)
                    tile = jnp.where((lane >= tt * c) & (lane < (tt + 1) * c), moved, tile)
                lo = gi * width + j * LANES
                u_scr[:, lo:lo + LANES] = pltpu.bitcast(tile, BF16)
        s = jnp.dot(u_scr[...], w_ref[pi], preferred_element_type=F32)
        sre_scr[...] = s[:, :half]
        sim_scr[...] = s[:, half:]
        a_re, a_im = are_ref[pi], aim_ref[pi]
        h0 = h0_ref[pi, 0]

        def step(k, carry):
            h_re, h_im = carry
            rows = pl.ds(k, nb, stride=nk)
            hre_scr[rows, :] = h_re
            him_scr[rows, :] = h_im
            return (a_re * h_re - a_im * h_im + sre_scr[rows, :],
                    a_re * h_im + a_im * h_re + sim_scr[rows, :])

        h_re, h_im = lax.fori_loop(0, nk, step, (h0[:, :half], h0[:, half:]))
        hl_ref[pi, 0, :, :half] = h_re
        hl_ref[pi, 0, :, half:] = h_im
        hin = jnp.concatenate([hre_scr[...], him_scr[...]], axis=1).astype(BF16)
        for gi in range(2):
            sl = slice(gi * width, (gi + 1) * width)
            y = (jnp.dot(u_scr[:, sl], m_ref[pi, gi], preferred_element_type=F32)
                 + jnp.dot(hin, cp_ref[pi, gi], preferred_element_type=F32))
            y_scr[pi, :, sl] = y.astype(y_scr.dtype)

    for t in range(tc):
        tile = jnp.zeros(lane.shape, jnp.int32)
        for pi in range(n_pairs):
            for gi in range(2):
                lo = gi * width + (t // per_tile) * LANES
                src = pltpu.bitcast(y_scr[pi, :, lo:lo + LANES], jnp.int32)
                dst = (2 * pi + gi) * c
                moved = _move_lanes(src, (t % per_tile) * c, dst)
                tile = jnp.where((lane >= dst) & (lane < dst + c), moved, tile)
        y_ref[t] = pltpu.bitcast(tile, BF16)


def _s5_scan(bt, u_t, ops, h0_re, h0_im):
    w, m, cp, are, aim = ops
    tc, rows, d = u_t.shape
    nk = rows // bt
    nbs = min(bt, S5_SEQS_PER_STEP if nk > 1 else bt)
    nsteps = bt // nbs
    gp = w.shape[0]
    p = are.shape[-1] // 2
    pairs = LANES // (2 * SSM_GROUP_CH)
    wide = 2 * tc * SSM_GROUP_CH
    rblk = nbs * nk

    def pair(x):
        return x.astype(F32).reshape(nsteps, nbs, gp, 2 * p).transpose(2, 0, 1, 3)

    h0 = jnp.concatenate([pair(h0_re), pair(h0_im)], axis=-1)
    y_t, hl = pl.pallas_call(
        functools.partial(_s5_kernel, nk=nk, nb=nbs),
        out_shape=(jax.ShapeDtypeStruct((tc, rows, d), BF16), jax.ShapeDtypeStruct((gp, nsteps, nbs, 4 * p), F32)),
        grid=(gp // pairs, nsteps),
        in_specs=[pl.BlockSpec((tc, rblk, LANES), lambda i, j: (0, j, i)),
                  pl.BlockSpec((pairs, wide, 4 * p), lambda i, j: (i, 0, 0)),
                  pl.BlockSpec((pairs, 2, wide // 2, wide // 2), lambda i, j: (i, 0, 0, 0)),
                  pl.BlockSpec((pairs, 2, 4 * p, wide // 2), lambda i, j: (i, 0, 0, 0)),
                  pl.BlockSpec((pairs, 1, 2 * p), lambda i, j: (i, 0, 0)),
                  pl.BlockSpec((pairs, 1, 2 * p), lambda i, j: (i, 0, 0)),
                  pl.BlockSpec((pairs, 1, nbs, 4 * p), lambda i, j: (i, j, 0, 0))],
        out_specs=(pl.BlockSpec((tc, rblk, LANES), lambda i, j: (0, j, i)),
                   pl.BlockSpec((pairs, 1, nbs, 4 * p), lambda i, j: (i, j, 0, 0))),
        scratch_shapes=[pltpu.VMEM((rblk, wide), BF16)] + [pltpu.VMEM((rblk, 2 * p), F32)] * 4
        + [pltpu.VMEM((pairs, rblk, wide), BF16)],
        compiler_params=_params("parallel", "parallel"),
        name="s5_scan",
    )(u_t, w, m, cp, are, aim, h0)

    def unpair(x):
        return x.transpose(1, 2, 0, 3).reshape(bt, 2 * gp, p)

    return y_t, unpair(hl[..., :2 * p]), unpair(hl[..., 2 * p:])


def _row_permutation(outer, inner):
    n = outer * inner
    assert inner & (inner - 1) == 0
    row = lax.broadcasted_iota(jnp.int32, (n, n), 0)
    col = lax.broadcasted_iota(jnp.int32, (n, n), 1)
    src = (row & (inner - 1)) * outer + (row >> (inner.bit_length() - 1))
    return jnp.where(col == src, 1.0, 0.0).astype(BF16)


def _in_proj_kernel(x_ref, nw_ref, sh_ref, sc_ref, w_ref, o_ref):
    tc, rl = o_ref.shape[0], o_ref.shape[1]
    xm = _norm_mod(x_ref[...], nw_ref[...], sh_ref[0], sc_ref[0])
    u = jnp.dot(xm.astype(BF16), w_ref[...], preferred_element_type=F32).astype(BF16)
    ut = jnp.dot(_row_permutation(tc, rl), u, preferred_element_type=F32).astype(o_ref.dtype)
    for t in range(tc):
        o_ref[t] = ut[t * rl:(t + 1) * rl]


def _s5_in_proj(grp, h, nw, shift, scale, w_bf):
    d = h.shape[-1]
    tc = S5_CHUNK
    if not grp.per_batch:
        u = _norm_linear(grp, h, nw, shift, scale, w_bf, "s5_in_proj")
        u3 = jnp.pad(u.reshape(grp.bt, grp.seq, d), ((0, 0), (tc - grp.seq, 0), (0, 0)))
        return u3.transpose(1, 0, 2)
    rl = grp.tm // tc
    sh, sh_spec = grp.mod(shift)
    sc, sc_spec = grp.mod(scale)
    return pl.pallas_call(
        _in_proj_kernel,
        out_shape=jax.ShapeDtypeStruct((tc, grp.t // tc, d), BF16),
        grid=(grp.steps,),
        in_specs=[grp.rows(d), _const((1, d)), sh_spec, sc_spec, _const((d, d))],
        out_specs=pl.BlockSpec((tc, rl, d), lambda i: (0, i, 0)),
        compiler_params=_params("parallel"),
        name="s5_in_proj",
    )(h, nw.reshape(1, d), sh, sc, w_bf)


def _glu_kernel(y_ref, u_ref, ds_ref, wa_ref, wb_ref, h_ref, g_ref, o_ref):
    z = y_ref[...].astype(F32) + ds_ref[...] * u_ref[...].astype(F32)
    g = jax.nn.gelu(z).astype(BF16)
    a = jnp.dot(g, wa_ref[...], preferred_element_type=F32)
    b = jnp.dot(g, wb_ref[...], preferred_element_type=F32)
    o_ref[...] = h_ref[...] + g_ref[0] * (a * jax.nn.sigmoid(b))


def _glu_chunk_kernel(y_ref, u_ref, ds_ref, wa_ref, wb_ref, h_ref, g_ref, o_ref):
    tc, rl, d = y_ref.shape
    z = (y_ref[...].reshape(tc * rl, d).astype(F32)
         + ds_ref[...] * u_ref[...].reshape(tc * rl, d).astype(F32))
    g = jax.nn.gelu(z).astype(BF16)
    g = jnp.dot(_row_permutation(rl, tc), g, preferred_element_type=F32).astype(BF16)
    a = jnp.dot(g, wa_ref[...], preferred_element_type=F32)
    b = jnp.dot(g, wb_ref[...], preferred_element_type=F32)
    o_ref[...] = h_ref[...] + g_ref[0] * (a * jax.nn.sigmoid(b))


def _glu(grp, y_t, u_t, d_skip, wa_bf, wb_bf, h, gate):
    d = h.shape[-1]
    tc = y_t.shape[0]
    g, g_spec = grp.mod(gate)
    if grp.per_batch:
        chunk_spec = pl.BlockSpec((tc, grp.tm // tc, d), lambda i: (0, i, 0))
        return pl.pallas_call(
            _glu_chunk_kernel,
            out_shape=jax.ShapeDtypeStruct((grp.t, d), F32),
            grid=(grp.steps,),
            in_specs=[chunk_spec, chunk_spec, _const((1, d)), _const((d, d)), _const((d, d)), grp.rows(d), g_spec],
            out_specs=grp.rows(d),
            compiler_params=_params("parallel"),
            name="s5_glu",
        )(y_t, u_t, d_skip.astype(F32).reshape(1, d), wa_bf, wb_bf, h, g)
    y = y_t.transpose(1, 0, 2)[:, tc - grp.seq:, :].reshape(grp.t, d)
    u = u_t.transpose(1, 0, 2)[:, tc - grp.seq:, :].reshape(grp.t, d)
    return pl.pallas_call(
        _glu_kernel,
        out_shape=jax.ShapeDtypeStruct((grp.t, d), F32),
        grid=(grp.steps,),
        in_specs=[grp.rows(d), grp.rows(d), _const((1, d)), _const((d, d)), _const((d, d)), grp.rows(d), g_spec],
        out_specs=grp.rows(d),
        compiler_params=_params("parallel"),
        name="s5_glu",
    )(y, u, d_skip.astype(F32).reshape(1, d), wa_bf, wb_bf, h, g)


def _router_kernel(x_ref, nw_ref, sh_ref, sc_ref, wr_ref, br_ref, xf_ref, route_ref):
    xm = _norm_mod(x_ref[...], nw_ref[...], sh_ref[0], sc_ref[0])
    xf_ref[...] = xm
    logits = jnp.dot(xm, wr_ref[...], preferred_element_type=F32, precision=HIGHEST) + br_ref[...]
    lane = lax.broadcasted_iota(jnp.int32, logits.shape, 1).astype(F32)
    big = float(ROUTE_LANES)
    is_g = lane < N_EGROUPS
    gmax = jnp.max(jnp.where(is_g, logits, NEG_INF), axis=-1, keepdims=True)
    gsum = jnp.sum(jnp.where(is_g, jnp.exp(logits - gmax), 0.0), axis=-1, keepdims=True)
    g_sel = jnp.min(jnp.where(is_g & (logits == gmax), lane, big), axis=-1, keepdims=True)
    g_gate = 1.0 / gsum
    lo = N_EGROUPS + EXPERTS_PER_GROUP * g_sel
    is_e = (lane >= lo) & (lane < lo + EXPERTS_PER_GROUP)
    emax = jnp.max(jnp.where(is_e, logits, NEG_INF), axis=-1, keepdims=True)
    ex = jnp.where(is_e, jnp.exp(logits - emax), -1.0)
    p1 = jnp.max(ex, axis=-1, keepdims=True)
    i1 = jnp.min(jnp.where(ex == p1, lane, big), axis=-1, keepdims=True)
    ex2 = jnp.where(lane == i1, -1.0, ex)
    p2 = jnp.max(ex2, axis=-1, keepdims=True)
    i2 = jnp.min(jnp.where(ex2 == p2, lane, big), axis=-1, keepdims=True)
    inv = g_gate / (p1 + p2)
    rec = jnp.where(lane == 0, p1 * inv, 0.0)
    rec = jnp.where(lane == 1, p2 * inv, rec)
    rec = jnp.where(lane == 2, i1 - N_EGROUPS, rec)
    rec = jnp.where(lane == 3, i2 - N_EGROUPS, rec)
    route_ref[...] = rec


def _router(grp, h, nw, shift, scale, wr, br):
    d = h.shape[-1]
    sh, sh_spec = grp.mod(shift)
    sc, sc_spec = grp.mod(scale)
    return pl.pallas_call(
        _router_kernel,
        out_shape=(jax.ShapeDtypeStruct((grp.t, d), F32), jax.ShapeDtypeStruct((grp.t, ROUTE_LANES), F32)),
        grid=(grp.steps,),
        in_specs=[grp.rows(d), _const((1, d)), sh_spec, sc_spec, _const((d, ROUTE_LANES)), _const((1, ROUTE_LANES))],
        out_specs=(grp.rows(d), grp.rows(ROUTE_LANES)),
        compiler_params=_params("parallel"),
        name="moe_router",
    )(h, nw.reshape(1, d), sh, sc, wr, br)


def _moe_plan(route, bm):
    t = route.shape[0]
    a = 2 * t
    blk = math.gcd(a, LANES)
    e = route[:, 2:4].astype(jnp.int32).reshape(a)
    oh = e[:, None] == jnp.arange(N_EXPERTS, dtype=jnp.int32)[None, :]
    tri = jnp.tril(jnp.ones((blk, blk), BF16))
    within = jnp.einsum('ij,bjk->bik', tri, oh.astype(BF16).reshape(a // blk, blk, N_EXPERTS),
                        preferred_element_type=F32)
    tot = within[:, -1, :]
    offs = jnp.cumsum(tot, axis=0) - tot
    rank = (within + offs[:, None, :]).reshape(a, N_EXPERTS)
    counts = (offs[-1] + tot[-1]).astype(jnp.int32)
    padded = (counts + bm - 1) // bm * bm
    pad_end = jnp.cumsum(padded)
    pad_start = pad_end - padded
    ohf = oh.astype(F32)
    dest = jnp.sum(ohf * (rank - 1.0 + pad_start.astype(F32)[None, :]), axis=-1).astype(jnp.int32)
    n_blocks = a // bm + N_EXPERTS
    first_row = jnp.arange(n_blocks, dtype=jnp.int32) * bm
    block_expert = jnp.minimum(jnp.sum((pad_end[None, :] <= first_row[:, None]).astype(jnp.int32), axis=-1),
                               N_EXPERTS - 1)
    block_live = (first_row < pad_end[-1]).astype(jnp.int32)
    return dest, block_expert, block_live, n_blocks


def _row_copies(tm, copy):
    def issue(i, carry):
        for u in range(ROW_UNROLL):
            copy(i * ROW_UNROLL + u, 0).start()
            copy(i * ROW_UNROLL + u, 1).start()
        return carry

    def drain(i, carry):
        for u in range(ROW_UNROLL):
            copy(i * ROW_UNROLL + u, 0).wait()
            copy(i * ROW_UNROLL + u, 1).wait()
        return carry

    lax.fori_loop(0, tm // ROW_UNROLL, issue, 0)
    lax.fori_loop(0, tm // ROW_UNROLL, drain, 0)


def _dispatch_kernel(dest_ref, x_ref, xb_in, xb_out, sem):
    del xb_in
    tm = x_ref.shape[0]

    def copy(r, k):
        dst = dest_ref[0, 0, 2 * r + k]
        return pltpu.make_async_copy(x_ref.at[pl.ds(r, 1)], xb_out.at[pl.ds(dst, 1)], sem)

    _row_copies(tm, copy)


def _dispatch(grp, xf, dest, n_rows):
    d = xf.shape[-1]
    tm = min(grp.tm, 256)
    steps = grp.t // tm
    return pl.pallas_call(
        _dispatch_kernel,
        out_shape=jax.ShapeDtypeStruct((n_rows, d), F32),
        grid=(steps,),
        in_specs=[pl.BlockSpec((1, 1, 2 * tm), lambda i: (i, 0, 0), memory_space=pltpu.SMEM),
                  pl.BlockSpec((tm, d), lambda i: (i, 0)),
                  pl.BlockSpec(memory_space=pl.ANY)],
        out_specs=pl.BlockSpec(memory_space=pl.ANY),
        scratch_shapes=[pltpu.SemaphoreType.DMA],
        input_output_aliases={2: 0},
        compiler_params=_params("arbitrary"),
        name="moe_dispatch",
    )(dest.reshape(steps, 1, 2 * tm), xf, jnp.zeros((n_rows, d), F32))


def _expert_kernel(be_ref, live_ref, x_ref, w1_ref, w3_ref, w2_ref, o_ref):
    del be_ref
    i = pl.program_id(0)

    @pl.when(live_ref[i] > 0)
    def _():
        x = x_ref[...].astype(BF16)
        a = jnp.dot(x, w1_ref[0], preferred_element_type=F32)
        b = jnp.dot(x, w3_ref[0], preferred_element_type=F32)
        hid = (a * jax.nn.sigmoid(a) * b).astype(BF16)
        o_ref[...] = jnp.dot(hid, w2_ref[0], preferred_element_type=F32)

    @pl.when(live_ref[i] == 0)
    def _():
        o_ref[...] = jnp.zeros_like(o_ref)


def _experts(xb, block_expert, block_live, bm, w1_bf, w3_bf, w2_bf):
    n_rows, d = xb.shape
    f = w1_bf.shape[-1]
    return pl.pallas_call(
        _expert_kernel,
        out_shape=jax.ShapeDtypeStruct((n_rows, d), F32),
        grid_spec=pltpu.PrefetchScalarGridSpec(
            num_scalar_prefetch=2,
            grid=(n_rows // bm,),
            in_specs=[pl.BlockSpec((bm, d), lambda i, be, lv: (i, 0)),
                      pl.BlockSpec((1, d, f), lambda i, be, lv: (be[i], 0, 0)),
                      pl.BlockSpec((1, d, f), lambda i, be, lv: (be[i], 0, 0)),
                      pl.BlockSpec((1, f, d), lambda i, be, lv: (be[i], 0, 0))],
            out_specs=pl.BlockSpec((bm, d), lambda i, be, lv: (i, 0))),
        compiler_params=_params("arbitrary"),
        name="moe_experts",
    )(block_expert, block_live, xb, w1_bf, w3_bf, w2_bf)


def _combine_kernel(dest_ref, h_ref, g_ref, route_ref, yb_hbm, o_ref, gbuf, sem):
    tm = h_ref.shape[0]

    def copy(r, k):
        src = dest_ref[0, 0, 2 * r + k]
        return pltpu.make_async_copy(yb_hbm.at[pl.ds(src, 1)], gbuf.at[k, pl.ds(r, 1)], sem)

    _row_copies(tm, copy)
    rec = route_ref[...]
    o_ref[...] = h_ref[...] + g_ref[0] * (rec[:, 0:1] * gbuf[0] + rec[:, 1:2] * gbuf[1])


def _combine(grp, h, gate, route, dest, yb):
    d = h.shape[-1]
    tm = min(grp.tm, 256)
    steps = grp.t // tm
    if grp.per_batch:
        per = grp.seq // tm
        g, g_spec = gate[:, None, :], pl.BlockSpec((1, 1, d), lambda i: (i // per, 0, 0))
    else:
        g, g_spec = grp.mod(gate)
    return pl.pallas_call(
        _combine_kernel,
        out_shape=jax.ShapeDtypeStruct((grp.t, d), F32),
        grid=(steps,),
        in_specs=[pl.BlockSpec((1, 1, 2 * tm), lambda i: (i, 0, 0), memory_space=pltpu.SMEM),
                  pl.BlockSpec((tm, d), lambda i: (i, 0)),
                  g_spec,
                  pl.BlockSpec((tm, ROUTE_LANES), lambda i: (i, 0)),
                  pl.BlockSpec(memory_space=pl.ANY)],
        out_specs=pl.BlockSpec((tm, d), lambda i: (i, 0)),
        scratch_shapes=[pltpu.VMEM((2, tm, d), F32), pltpu.SemaphoreType.DMA],
        compiler_params=_params("arbitrary"),
        name="moe_combine",
    )(dest.reshape(steps, 1, 2 * tm), h, g, route, yb)


def _moe(grp, h, nw, shift, scale, gate, wr, br, w1_bf, w3_bf, w2_bf):
    bm = 256 if grp.t >= 2048 else 32
    xf, route = _router(grp, h, nw, shift, scale, wr, br)
    dest, block_expert, block_live, n_blocks = _moe_plan(route, bm)
    xb = _dispatch(grp, xf, dest, n_blocks * bm)
    yb = _experts(xb, block_expert, block_live, bm, w1_bf, w3_bf, w2_bf)
    return _combine(grp, h, gate, route, dest, yb)


def _attn_kernel(lam_ref, q_ref, k_ref, v_ref, qnw_ref, sub_ref, o_ref, qz_scr, m_scr, l_scr, acc_scr, kmax_scr,
                 *, out_scale, online):
    tq = q_ref.shape[1]
    tk = tq
    rc = min(ATTN_ROWS, tq)
    qi = pl.program_id(2)
    qn = _half_norm(q_ref[0].astype(F32), qnw_ref[...], HEAD_DIM ** -0.5 * LOG2E)
    low = lax.broadcasted_iota(jnp.int32, qn.shape, 1) < HEAD_DIM
    qz = jnp.concatenate([jnp.where(low, qn, 0.0), jnp.where(low, 0.0, qn)], axis=0).astype(BF16)
    qz_scr[...] = qz
    l_scr[...] = jnp.zeros_like(l_scr)
    acc_scr[...] = jnp.zeros_like(acc_scr)
    if online:
        m_scr[...] = jnp.full_like(m_scr, NEG_INF)
    else:
        @pl.when(qi == 0)
        def _():
            kf = k_ref[0].astype(F32)
            klow = lax.broadcasted_iota(jnp.int32, kf.shape, 1) < HEAD_DIM
            sq = kf * kf
            n1 = jnp.max(jnp.sum(jnp.where(klow, sq, 0.0), axis=-1, keepdims=True), axis=0, keepdims=True)
            n2 = jnp.max(jnp.sum(jnp.where(klow, 0.0, sq), axis=-1, keepdims=True), axis=0, keepdims=True)
            kmax_scr[0:1, :] = jnp.broadcast_to(jnp.sqrt(n1), (1, V_DIM))
            kmax_scr[1:2, :] = jnp.broadcast_to(jnp.sqrt(n2), (1, V_DIM))

        qf = qz.astype(F32)
        qnorm = jnp.sqrt(jnp.sum(qf * qf, axis=-1, keepdims=True))
        m_scr[0:tq, :] = qnorm[:tq] * kmax_scr[0:1, :]
        m_scr[tq:, :] = qnorm[tq:] * kmax_scr[1:2, :]

    def tile(j, masked):
        r0 = pl.multiple_of(j * tk, tk)
        for c in range(2 * tq // rc):
            rows = slice(c * rc, (c + 1) * rc)
            q0 = (c * rc) % tq
            nk = min(tk, q0 + rc) if masked else tk
            k = k_ref[0, pl.ds(r0, nk), :]
            v = v_ref[0, pl.ds(r0, nk), :]
            s = lax.dot_general(qz_scr[rows, :], k, (((1,), (1,)), ((), ())), preferred_element_type=F32)
            if masked:
                row = lax.broadcasted_iota(jnp.int32, (rc, nk), 0) + q0
                col = lax.broadcasted_iota(jnp.int32, (rc, nk), 1)
                s = jnp.where(col <= row, s, NEG_INF)
            parts = [s[:, t * LANES:(t + 1) * LANES] for t in range(nk // LANES)]
            if online:
                m_old = m_scr[rows, :]
                mx = functools.reduce(jnp.maximum, parts)
                m_new = jnp.maximum(m_old, jnp.max(mx, axis=-1, keepdims=True))
                alpha = jnp.exp2(m_old - m_new)
                m_scr[rows, :] = m_new
            else:
                m_new = m_scr[rows, :]
            ps = [jnp.exp2(x - m_new) for x in parts]
            lsum = functools.reduce(lambda a, b: a + b, ps)
            pv = jnp.dot(jnp.concatenate(ps, axis=1).astype(BF16), v, preferred_element_type=F32)
            if online:
                l_scr[rows, :] = alpha * l_scr[rows, :] + lsum
                acc_scr[rows, :] = alpha * acc_scr[rows, :] + pv
            else:
                l_scr[rows, :] = l_scr[rows, :] + lsum
                acc_scr[rows, :] = acc_scr[rows, :] + pv

    def full(j, carry):
        tile(j, False)
        return carry

    lax.fori_loop(0, qi, full, 0)
    tile(qi, True)
    acc = acc_scr[...] / jnp.sum(l_scr[...], axis=-1, keepdims=True)
    o = acc[:tq] - lam_ref[0] * acc[tq:]
    o = o * lax.rsqrt(jnp.mean(o * o, axis=-1, keepdims=True) + RMS_EPS) * sub_ref[...] * out_scale
    o_ref[0] = o.astype(o_ref.dtype)


def _attn_prompt(grp, q, kb, vb, lam, qnw, knw, subw, out_scale):
    bt, seq, d = grp.bt, grp.seq, q.shape[-1]
    nh = d // V_DIM
    tq = min(ATTN_TILE, seq)
    nq = seq // tq

    def run(online):
        return pl.pallas_call(
            functools.partial(_attn_kernel, out_scale=out_scale, online=online),
            out_shape=jax.ShapeDtypeStruct((bt, seq, d), BF16),
            grid_spec=pltpu.PrefetchScalarGridSpec(
                num_scalar_prefetch=1,
                grid=(bt, nh, nq),
                in_specs=[pl.BlockSpec((1, tq, V_DIM), lambda b, h, i, lam: (b, i, h)),
                          pl.BlockSpec((1, seq, V_DIM), lambda b, h, i, lam: (b, 0, h)),
                          pl.BlockSpec((1, seq, V_DIM), lambda b, h, i, lam: (b, 0, h)),
                          pl.BlockSpec((1, V_DIM), lambda b, h, i, lam: (0, 0)),
                          pl.BlockSpec((1, V_DIM), lambda b, h, i, lam: (0, 0))],
                out_specs=pl.BlockSpec((1, tq, V_DIM), lambda b, h, i, lam: (b, i, h)),
                scratch_shapes=[pltpu.VMEM((2 * tq, V_DIM), BF16), pltpu.VMEM((2 * tq, LANES), F32),
                                pltpu.VMEM((2 * tq, LANES), F32), pltpu.VMEM((2 * tq, V_DIM), F32),
                                pltpu.VMEM((SUBLANES, V_DIM), F32)]),
            compiler_params=_params("parallel", "parallel", "arbitrary"),
            name="diff_attn_prompt_online" if online else "diff_attn_prompt",
        )(lam.reshape(1), q.reshape(bt, seq, d), kb.reshape(bt, seq, d), vb.reshape(bt, seq, d), qnw, subw)

    bound = (HEAD_DIM ** 0.5) * jnp.max(jnp.abs(qnw)) * jnp.max(jnp.abs(knw)) * LOG2E
    out = lax.cond(bound <= FIXED_SHIFT_MAX, lambda: run(False), lambda: run(True))
    return out.reshape(grp.t, d)


def _paged_kernel(pt_ref, lam_ref, q_ref, kn_ref, vn_ref, qnw_ref, sub_ref, ck_hbm, cv_hbm, o_ref,
                  kbuf, vbuf, sem, *, n_chunks, out_scale):
    b = pl.program_id(0)
    nb = pl.num_programs(0)
    pps = kbuf.shape[1]
    page = kbuf.shape[2] // (q_ref.shape[-1] // V_DIM)
    t = q_ref.shape[1]
    nh = q_ref.shape[-1] // V_DIM

    def copies(seq_i, chunk, slot):
        out = []
        for pg in range(pps):
            phys = pt_ref[seq_i, chunk * pps + pg]
            out.append(pltpu.make_async_copy(ck_hbm.at[phys], kbuf.at[slot, pg], sem.at[slot, 0]))
            out.append(pltpu.make_async_copy(cv_hbm.at[phys], vbuf.at[slot, pg], sem.at[slot, 1]))
        return out

    @pl.when(b == 0)
    def _():
        for cp in copies(0, 0, 0):
            cp.start()

    def heads(ref):
        return jnp.stack([ref[0, :, hd * V_DIM:(hd + 1) * V_DIM] for hd in range(nh)], axis=0)

    qn = _half_norm(heads(q_ref).astype(F32), qnw_ref[...], HEAD_DIM ** -0.5)
    low = lax.broadcasted_iota(jnp.int32, qn.shape, 2) < HEAD_DIM
    qz = jnp.concatenate([jnp.where(low, qn, 0.0), jnp.where(low, 0.0, qn)], axis=1)
    qzb = qz.astype(BF16)

    def update(state, s, pv):
        m_old, l_old, acc = state
        m_new = jnp.maximum(m_old, jnp.max(s, axis=-1, keepdims=True))
        corr = jnp.exp(m_old - m_new)
        p = jnp.exp(s - m_new)
        return m_new, l_old * corr + jnp.sum(p, axis=-1, keepdims=True), acc * corr + pv(p)

    def gather_heads(buf, slot):
        return jnp.stack([
            jnp.concatenate([buf[slot, pg, pl.ds(hd, page, stride=nh), :] for pg in range(pps)], axis=0)
            for hd in range(nh)], axis=0).astype(BF16)

    def chunk_step(c, state):
        it = b * n_chunks + c
        slot = lax.rem(it, 2)
        nxt = it + 1
        nseq = nxt // n_chunks

        @pl.when(nseq < nb)
        def _():
            for cp in copies(nseq, nxt - nseq * n_chunks, 1 - slot):
                cp.start()

        for cp in copies(b, c, slot):
            cp.wait()
        kh = gather_heads(kbuf, slot)
        vh = gather_heads(vbuf, slot)
        s = jnp.einsum('hqd,hkd->hqk', qzb, kh, preferred_element_type=F32)
        return update(state, s, lambda p: jnp.einsum('hqk,hkd->hqd', p.astype(BF16), vh,
                                                     preferred_element_type=F32))

    init = (jnp.full((nh, 2 * t, 1), NEG_INF, F32), jnp.zeros((nh, 2 * t, 1), F32),
            jnp.zeros((nh, 2 * t, V_DIM), F32))
    state = lax.fori_loop(0, n_chunks, chunk_step, init)

    kn, vn = heads(kn_ref), heads(vn_ref)
    row = lax.broadcasted_iota(jnp.int32, (nh, 2 * t, t), 1)
    col = lax.broadcasted_iota(jnp.int32, (nh, 2 * t, t), 2)
    s = jnp.concatenate([jnp.sum(qz * kn[:, j:j + 1, :], axis=-1, keepdims=True) for j in range(t)], axis=-1)
    s = jnp.where(col <= jnp.where(row >= t, row - t, row), s, NEG_INF)

    def pv_new(p):
        out = p[:, :, 0:1] * vn[:, 0:1, :]
        for j in range(1, t):
            out = out + p[:, :, j:j + 1] * vn[:, j:j + 1, :]
        return out

    _, l_fin, acc = update(state, s, pv_new)
    acc = acc / l_fin
    o = acc[:, :t] - lam_ref[0] * acc[:, t:]
    o = o * lax.rsqrt(jnp.mean(o * o, axis=-1, keepdims=True) + RMS_EPS) * sub_ref[...] * out_scale
    for hd in range(nh):
        o_ref[0, :, hd * V_DIM:(hd + 1) * V_DIM] = o[hd].astype(o_ref.dtype)


def _attn_sample(grp, q, k_new, v_new, cache_k, cache_v, page_table, lam, qnw, subw, out_scale):
    bd, t, d = grp.bt, grp.seq, q.shape[-1]
    n_phys, page, nh, vd = cache_k.shape
    n_pages = page_table.shape[1]
    pps = math.gcd(PAGES_PER_STEP, n_pages)
    n_chunks = n_pages // pps
    ck = cache_k.reshape(n_phys, page * nh, vd)
    cv = cache_v.reshape(n_phys, page * nh, vd)
    seq_spec = pl.BlockSpec((1, t, d), lambda b, pt, lam: (b, 0, 0))
    out = pl.pallas_call(
        functools.partial(_paged_kernel, n_chunks=n_chunks, out_scale=out_scale),
        out_shape=jax.ShapeDtypeStruct((bd, t, d), BF16),
        grid_spec=pltpu.PrefetchScalarGridSpec(
            num_scalar_prefetch=2,
            grid=(bd,),
            in_specs=[seq_spec, seq_spec, seq_spec,
                      pl.BlockSpec((1, vd), lambda b, pt, lam: (0, 0)),
                      pl.BlockSpec((1, vd), lambda b, pt, lam: (0, 0)),
                      pl.BlockSpec(memory_space=pl.ANY),
                      pl.BlockSpec(memory_space=pl.ANY)],
            out_specs=seq_spec,
            scratch_shapes=[pltpu.VMEM((2, pps, page * nh, vd), F32),
                            pltpu.VMEM((2, pps, page * nh, vd), F32),
                            pltpu.SemaphoreType.DMA((2, 2))]),
        compiler_params=_params("arbitrary"),
        name="diff_attn_paged",
    )(page_table.astype(jnp.int32), lam.reshape(1), q.reshape(bd, t, d), k_new.reshape(bd, t, d),
      v_new.reshape(bd, t, d), qnw, subw, ck, cv)
    return out.reshape(grp.t, d)


def _out_proj_kernel(o_ref, w_ref, h_ref, g_ref, y_ref):
    y_ref[...] = h_ref[...] + g_ref[0] * jnp.dot(o_ref[...], w_ref[...], preferred_element_type=F32)


def _out_proj(grp, o, w_bf, h, gate):
    d = h.shape[-1]
    g, g_spec = grp.mod(gate)
    return pl.pallas_call(
        _out_proj_kernel,
        out_shape=jax.ShapeDtypeStruct((grp.t, d), F32),
        grid=(grp.steps,),
        in_specs=[grp.rows(d), _const((d, d)), grp.rows(d), g_spec],
        out_specs=grp.rows(d),
        compiler_params=_params("parallel"),
        name="attn_out_proj",
    )(o, w_bf, h, g)


def _trunk(x, mods, mod_kv, h0_re, h0_im, kv_past, wts):
    bt, seq, d = x.shape
    grp = _Group(bt, seq)
    h = x.astype(F32).reshape(grp.t, d)
    depth = len(mods)
    n_a = depth // 2
    new_re, new_im = [], []
    k = v = kb = vb = None
    for l in range(depth):
        if l == n_a:
            k, v, kb, vb = _kv_proj(grp, h, wts['norm_kv_w'], mod_kv[:, :d], mod_kv[:, d:], wts['w_kv'],
                                    wts['k_norm_w'])
        sh1, sc1, g1, sh2, sc2, g2 = jnp.split(mods[l], 6, axis=-1)
        if l < n_a:
            assert seq % S5_CHUNK == 0 or seq < S5_CHUNK
            u_t = _s5_in_proj(grp, h, wts['norm_mix_w'][l], sh1, sc1, wts['ssm_w_in'][l])
            ops = _s5_operators(wts['ssm_a_re'][l], wts['ssm_a_im'][l], wts['ssm_log_dt'][l], wts['ssm_b_re'][l],
                                wts['ssm_b_im'][l], wts['ssm_c_re'][l], wts['ssm_c_im'][l], min(seq, S5_CHUNK))
            y_t, hl_re, hl_im = _s5_scan(bt, u_t, ops, h0_re[l], h0_im[l])
            new_re.append(hl_re)
            new_im.append(hl_im)
            h = _glu(grp, y_t, u_t, wts['ssm_d_skip'][l], wts['ssm_w_glu_a'][l], wts['ssm_w_glu_b'][l], h, g1)
        else:
            j = l - n_a
            q = _norm_linear(grp, h, wts['norm_mix_w'][l], sh1, sc1, wts['attn_w_q'][j], "attn_q_proj")
            lam_init = 0.8 - 0.6 * math.exp(-0.3 * l)
            f32 = F32
            lam = (jnp.exp(jnp.sum(wts['lambda_q1'][j].astype(f32) * wts['lambda_k1'][j].astype(f32)))
                   - jnp.exp(jnp.sum(wts['lambda_q2'][j].astype(f32) * wts['lambda_k2'][j].astype(f32))) + lam_init)
            qnw = jnp.tile(wts['q_norm_w'][j].astype(f32), V_DIM // HEAD_DIM).reshape(1, V_DIM)
            subw = wts['subln_w'][j].astype(f32).reshape(1, V_DIM)
            if kv_past is None:
                o = _attn_prompt(grp, q, kb, vb, lam, qnw, wts['k_norm_w'].astype(f32), subw, 1.0 - lam_init)
            else:
                o = _attn_sample(grp, q, k, v, kv_past[0], kv_past[1], kv_past[2], lam, qnw, subw, 1.0 - lam_init)
            h = _out_proj(grp, o, wts['attn_w_o'][j], h, g1)
        h = _moe(grp, h, wts['norm_ffn_w'][l], sh2, sc2, g2, wts['moe_wr'][l], wts['moe_br'][l],
                 wts['moe_w1'][l], wts['moe_w3'][l], wts['moe_w2'][l])
    nh = d // V_DIM
    return (h.reshape(bt, seq, d), k.reshape(bt, seq, nh, V_DIM), v.reshape(bt, seq, nh, V_DIM),
            jnp.stack(new_re), jnp.stack(new_im))


def kernel(x_prompt, x_sample, cache_k, cache_v, page_table, state_ssm_re, state_ssm_im, c_prompt, c_sample,
           w_mod, b_mod, norm_mix_w, norm_ffn_w,
           ssm_w_in, ssm_a_re, ssm_a_im, ssm_log_dt, ssm_b_re, ssm_b_im, ssm_c_re, ssm_c_im, ssm_d_skip, ssm_w_glu,
           w_mod_kv, b_mod_kv, norm_kv_w, w_kv, k_norm_w,
           attn_w_q, q_norm_w, lambda_q1, lambda_k1, lambda_q2, lambda_k2, subln_w, attn_w_o,
           moe_w_group, moe_b_group, moe_w_erouter, moe_b_erouter, moe_w1, moe_w3, moe_w2):
    d = x_prompt.shape[-1]
    depth = w_mod.shape[0]
    n_prompt = c_prompt.shape[0]
    c_all = jnp.concatenate([c_prompt, c_sample], axis=0).astype(F32)
    mods = [_mod_linear(c_all, w_mod[l], b_mod[l]) for l in range(depth)]
    mod_kv = _mod_linear(c_all, w_mod_kv, b_mod_kv)
    pad = ROUTE_LANES - N_EGROUPS - N_EXPERTS
    wr = jnp.concatenate([moe_w_group, moe_w_erouter.transpose(0, 2, 1, 3).reshape(depth, d, N_EXPERTS),
                          jnp.zeros((depth, d, pad), F32)], axis=-1)
    br = jnp.concatenate([moe_b_group, moe_b_erouter.reshape(depth, N_EXPERTS),
                          jnp.zeros((depth, pad), F32)], axis=-1).reshape(depth, 1, ROUTE_LANES)
    wts = dict(norm_mix_w=norm_mix_w, norm_ffn_w=norm_ffn_w, norm_kv_w=norm_kv_w, k_norm_w=k_norm_w,
               ssm_w_in=ssm_w_in.astype(BF16), ssm_a_re=ssm_a_re, ssm_a_im=ssm_a_im, ssm_log_dt=ssm_log_dt,
               ssm_b_re=ssm_b_re, ssm_b_im=ssm_b_im, ssm_c_re=ssm_c_re, ssm_c_im=ssm_c_im, ssm_d_skip=ssm_d_skip,
               ssm_w_glu_a=ssm_w_glu[..., :d].astype(BF16), ssm_w_glu_b=ssm_w_glu[..., d:].astype(BF16),
               w_kv=w_kv.astype(BF16), attn_w_q=attn_w_q.astype(BF16), q_norm_w=q_norm_w,
               lambda_q1=lambda_q1, lambda_k1=lambda_k1, lambda_q2=lambda_q2, lambda_k2=lambda_k2,
               subln_w=subln_w, attn_w_o=attn_w_o.astype(BF16), moe_wr=wr, moe_br=br,
               moe_w1=moe_w1.astype(BF16), moe_w3=moe_w3.astype(BF16), moe_w2=moe_w2.astype(BF16))
    zero = jnp.zeros((state_ssm_re.shape[0], n_prompt) + state_ssm_re.shape[2:], F32)
    y_p, k_p, v_p, re_p, im_p = _trunk(x_prompt, [m[:n_prompt] for m in mods], mod_kv[:n_prompt],
                                       zero, zero, None, wts)
    y_s, k_s, v_s, re_s, im_s = _trunk(x_sample, [m[n_prompt:] for m in mods], mod_kv[n_prompt:],
                                       state_ssm_re, state_ssm_im, (cache_k, cache_v, page_table), wts)
    return (y_p, y_s, k_p, v_p, k_s, v_s, re_p, im_p, re_s, im_s)
```

```python
import functools
import math

import jax
import jax.numpy as jnp
from jax import lax
from jax.experimental import pallas as pl
from jax.experimental.pallas import tpu as pltpu

F32 = jnp.float32
BF16 = jnp.bfloat16
HIGHEST = lax.Precision.HIGHEST

SSM_GROUP_CH = 16
HEAD_DIM = 64
V_DIM = 2 * HEAD_DIM
N_EGROUPS = 4
EXPERTS_PER_GROUP = 8
N_EXPERTS = N_EGROUPS * EXPERTS_PER_GROUP
RMS_EPS = 1e-6
NEG_INF = -1e30

LANES = 128
SUBLANES = 8
MXU_DIM = 256

S5_CHUNK = MXU_DIM // SSM_GROUP_CH
S5_SEQS_PER_STEP = 4
ROW_TILE = 512
ATTN_TILE = 512
ATTN_ROWS = 512
LOG2E = 1.4426950408889634
FIXED_SHIFT_MAX = 60.0
ROW_UNROLL = 8
PAGE_SLOTS = 3
PAGES_PER_STEP = 4
ROUTE_LANES = LANES
VMEM_LIMIT = 48 * 1024 * 1024


def _params(*sem):
    return pltpu.CompilerParams(dimension_semantics=sem, vmem_limit_bytes=VMEM_LIMIT)


def _norm_mod(x, nw, shift, scale):
    ms = jnp.mean(x * x, axis=-1, keepdims=True)
    return x * lax.rsqrt(ms + RMS_EPS) * nw * (1.0 + scale) + shift


def _half_norm(x, w, scale):
    lane = lax.broadcasted_iota(jnp.int32, x.shape, x.ndim - 1)
    low = lane < HEAD_DIM
    sq = x * x
    tot = jnp.sum(sq, axis=-1, keepdims=True)
    lo = jnp.sum(jnp.where(low, sq, 0.0), axis=-1, keepdims=True)
    ms = jnp.where(low, lo, tot - lo) * (1.0 / HEAD_DIM)
    return x * lax.rsqrt(ms + RMS_EPS) * w * scale


class _Group:
    def __init__(self, bt, seq):
        self.bt, self.seq, self.t = bt, seq, bt * seq
        if seq % ROW_TILE == 0:
            self.tm, self.per_batch = ROW_TILE, True
        else:
            assert self.t <= ROW_TILE and self.t % SUBLANES == 0
            self.tm, self.per_batch = self.t, False
        self.steps = self.t // self.tm

    def mod(self, vec):
        d = vec.shape[-1]
        if self.per_batch:
            per = self.seq // self.tm
            return vec[:, None, :], pl.BlockSpec((1, 1, d), lambda i: (i // per, 0, 0))
        arr = jnp.repeat(vec, self.seq, axis=0).reshape(self.steps, self.tm, d)
        return arr, pl.BlockSpec((1, self.tm, d), lambda i: (i, 0, 0))

    def rows(self, d):
        return pl.BlockSpec((self.tm, d), lambda i: (i, 0))


def _const(shape):
    return pl.BlockSpec(shape, lambda i: (0,) * len(shape))


def _mod_kernel(c_ref, w_ref, b_ref, o_ref):
    c = c_ref[...]
    a = c * jax.nn.sigmoid(c)
    o_ref[...] = jnp.dot(a, w_ref[0], preferred_element_type=F32, precision=HIGHEST) + b_ref[...]


def _mod_linear(c, w, layer, b):
    m, d = c.shape
    n = w.shape[-1]
    tn = 1024
    return pl.pallas_call(
        _mod_kernel,
        out_shape=jax.ShapeDtypeStruct((m, n), F32),
        grid=(n // tn,),
        in_specs=[pl.BlockSpec((m, d), lambda j: (0, 0)),
                  pl.BlockSpec((1, d, tn), lambda j: (layer, 0, j)),
                  pl.BlockSpec((1, tn), lambda j: (0, j))],
        out_specs=pl.BlockSpec((m, tn), lambda j: (0, j)),
        compiler_params=_params("parallel"),
        name="adaln_mod",
    )(c, w, b.reshape(1, n))


def _norm_linear_kernel(x_ref, nw_ref, sh_ref, sc_ref, w_ref, o_ref):
    xm = _norm_mod(x_ref[...], nw_ref[...], sh_ref[0], sc_ref[0])
    o_ref[...] = jnp.dot(xm.astype(BF16), w_ref[...], preferred_element_type=F32).astype(o_ref.dtype)


def _norm_linear(grp, h, nw, shift, scale, w_bf, name):
    d, n = w_bf.shape
    sh, sh_spec = grp.mod(shift)
    sc, sc_spec = grp.mod(scale)
    return pl.pallas_call(
        _norm_linear_kernel,
        out_shape=jax.ShapeDtypeStruct((grp.t, n), BF16),
        grid=(grp.steps,),
        in_specs=[grp.rows(d), _const((1, d)), sh_spec, sc_spec, _const((d, n))],
        out_specs=grp.rows(n),
        compiler_params=_params("parallel"),
        name=name,
    )(h, nw.reshape(1, d), sh, sc, w_bf)


def _kv_kernel(x_ref, nw_ref, sh_ref, sc_ref, w_ref, knw_ref, k_ref, v_ref, kb_ref, vb_ref):
    d = x_ref.shape[-1]
    xm = _norm_mod(x_ref[...], nw_ref[...], sh_ref[0], sc_ref[0])
    kv = jnp.dot(xm.astype(BF16), w_ref[...], preferred_element_type=F32)
    v = kv[:, d:]
    v_ref[...] = v
    vb_ref[...] = v.astype(BF16)
    for hd in range(d // V_DIM):
        sl = slice(hd * V_DIM, (hd + 1) * V_DIM)
        kn = _half_norm(kv[:, sl], knw_ref[...], 1.0)
        k_ref[:, sl] = kn
        kb_ref[:, sl] = kn.astype(BF16)


def _kv_proj(grp, h, nw, shift, scale, w_bf, k_norm_w):
    d = h.shape[-1]
    sh, sh_spec = grp.mod(shift)
    sc, sc_spec = grp.mod(scale)
    knw = jnp.tile(k_norm_w.astype(F32), V_DIM // HEAD_DIM).reshape(1, V_DIM)
    out = jax.ShapeDtypeStruct((grp.t, d), F32)
    outb = jax.ShapeDtypeStruct((grp.t, d), BF16)
    return pl.pallas_call(
        _kv_kernel,
        out_shape=(out, out, outb, outb),
        grid=(grp.steps,),
        in_specs=[grp.rows(d), _const((1, d)), sh_spec, sc_spec, _const((d, 2 * d)), _const((1, V_DIM))],
        out_specs=(grp.rows(d),) * 4,
        compiler_params=_params("parallel"),
        name="shared_kv",
    )(h, nw.reshape(1, d), sh, sc, w_bf, knw)


def _s5_operators(a_re, a_im, log_dt, b_re, b_im, c_re, c_im, n_valid):
    f32 = F32
    g, p = a_re.shape
    c = b_re.shape[-1]
    tc = S5_CHUNK
    a_re, a_im = a_re.astype(f32), a_im.astype(f32)
    dt = jnp.exp(log_dt.astype(f32))
    mag = jnp.exp(a_re * dt)
    lb_re, lb_im = mag * jnp.cos(a_im * dt), mag * jnp.sin(a_im * dt)
    den = a_re * a_re + a_im * a_im
    q_re = ((lb_re - 1.0) * a_re + lb_im * a_im) / den
    q_im = (lb_im * a_re - (lb_re - 1.0) * a_im) / den
    bb_re = q_re[..., None] * b_re - q_im[..., None] * b_im
    bb_im = q_re[..., None] * b_im + q_im[..., None] * b_re
    pw_re, pw_im = [jnp.ones_like(lb_re)], [jnp.zeros_like(lb_im)]
    for _ in range(tc):
        r, i = pw_re[-1], pw_im[-1]
        pw_re.append(r * lb_re - i * lb_im)
        pw_im.append(r * lb_im + i * lb_re)
    pw_re, pw_im = jnp.stack(pw_re), jnp.stack(pw_im)
    e_re = pw_re[:tc, :, :, None] * bb_re[None] - pw_im[:tc, :, :, None] * bb_im[None]
    e_im = pw_re[:tc, :, :, None] * bb_im[None] + pw_im[:tc, :, :, None] * bb_re[None]
    kt = (jnp.einsum('gcp,tgpd->tgdc', c_re, e_re, precision=HIGHEST)
          - jnp.einsum('gcp,tgpd->tgdc', c_im, e_im, precision=HIGHEST))
    s_idx = jnp.arange(tc)
    r0 = kt.transpose(1, 2, 0, 3).reshape(g, c, tc * c)
    m = jnp.stack([jnp.pad(r0, ((0, 0), (0, 0), (s * c, 0)))[:, :, :tc * c] for s in range(tc)], axis=1)
    m = m.reshape(g // 2, 2, tc * c, tc * c)
    w_re = e_re[::-1].transpose(1, 0, 3, 2).reshape(g // 2, 2, tc * c, p)
    w_im = e_im[::-1].transpose(1, 0, 3, 2).reshape(g // 2, 2, tc * c, p)
    z = jnp.zeros_like(w_re[:, 0])
    w = jnp.concatenate([
        jnp.concatenate([w_re[:, 0], z, w_im[:, 0], z], axis=-1),
        jnp.concatenate([z, w_re[:, 1], z, w_im[:, 1]], axis=-1)], axis=1)
    off = tc - n_valid
    tau = jnp.clip(s_idx - off + 1, 0, tc)
    live = (s_idx >= off)[:, None, None, None]
    d_re = jnp.where(live, c_re[None] * pw_re[tau][:, :, None, :] - c_im[None] * pw_im[tau][:, :, None, :], 0.0)
    d_im = jnp.where(live, c_re[None] * pw_im[tau][:, :, None, :] + c_im[None] * pw_re[tau][:, :, None, :], 0.0)
    cp_re = d_re.transpose(1, 3, 0, 2).reshape(g // 2, 2, p, tc * c)
    cp_im = -d_im.transpose(1, 3, 0, 2).reshape(g // 2, 2, p, tc * c)
    zc = jnp.zeros_like(cp_re[:, 0])
    cp = jnp.stack([
        jnp.concatenate([cp_re[:, 0], zc, cp_im[:, 0], zc], axis=1),
        jnp.concatenate([zc, cp_re[:, 1], zc, cp_im[:, 1]], axis=1)], axis=1)
    are = pw_re[n_valid].reshape(g // 2, 1, 2 * p)
    aim = pw_im[n_valid].reshape(g // 2, 1, 2 * p)
    return w.astype(BF16), m.astype(BF16), cp.astype(BF16), are, aim


def _move_lanes(x, src, dst):
    shift = (dst - src) % LANES
    return pltpu.roll(x, shift, axis=1) if shift else x


def _s5_kernel(u_ref, w_ref, m_ref, cp_ref, are_ref, aim_ref, h0_ref, y_ref, hl_ref,
               u_scr, sre_scr, sim_scr, hre_scr, him_scr, y_scr, *, nk, nb):
    tc = u_ref.shape[0]
    c = SSM_GROUP_CH
    per_tile = LANES // c
    half = sre_scr.shape[-1]
    width = tc * c
    n_pairs = w_ref.shape[0]
    lane = lax.broadcasted_iota(jnp.int32, (u_ref.shape[1] // 2, LANES), 1)

    for pi in range(n_pairs):
        for gi in range(2):
            for j in range(tc // per_tile):
                tile = jnp.zeros(lane.shape, jnp.int32)
                for tt in range(per_tile):
                    src = pltpu.bitcast(u_ref[j * per_tile + tt], jnp.int32)
                    moved = _move_lanes(src, (2 * pi + gi) * c, tt * c)
                    tile = jnp.where((lane >= tt * c) & (lane < (tt + 1) * c), moved, tile)
                lo = gi * width + j * LANES
                u_scr[:, lo:lo + LANES] = pltpu.bitcast(tile, BF16)
        s = jnp.dot(u_scr[...], w_ref[pi], preferred_element_type=F32)
        sre_scr[...] = s[:, :half]
        sim_scr[...] = s[:, half:]
        a_re, a_im = are_ref[pi], aim_ref[pi]
        h0 = h0_ref[pi, 0]

        def step(k, carry):
            h_re, h_im = carry
            rows = pl.ds(k, nb, stride=nk)
            hre_scr[rows, :] = h_re
            him_scr[rows, :] = h_im
            return (a_re * h_re - a_im * h_im + sre_scr[rows, :],
                    a_re * h_im + a_im * h_re + sim_scr[rows, :])

        h_re, h_im = lax.fori_loop(0, nk, step, (h0[:, :half], h0[:, half:]))
        hl_ref[pi, 0, :, :half] = h_re
        hl_ref[pi, 0, :, half:] = h_im
        hin = jnp.concatenate([hre_scr[...], him_scr[...]], axis=1).astype(BF16)
        for gi in range(2):
            sl = slice(gi * width, (gi + 1) * width)
            y = (jnp.dot(u_scr[:, sl], m_ref[pi, gi], preferred_element_type=F32)
                 + jnp.dot(hin, cp_ref[pi, gi], preferred_element_type=F32))
            y_scr[pi, :, sl] = y.astype(y_scr.dtype)

    for t in range(tc):
        tile = jnp.zeros(lane.shape, jnp.int32)
        for pi in range(n_pairs):
            for gi in range(2):
                lo = gi * width + (t // per_tile) * LANES
                src = pltpu.bitcast(y_scr[pi, :, lo:lo + LANES], jnp.int32)
                dst = (2 * pi + gi) * c
                moved = _move_lanes(src, (t % per_tile) * c, dst)
                tile = jnp.where((lane >= dst) & (lane < dst + c), moved, tile)
        y_ref[t] = pltpu.bitcast(tile, BF16)


def _s5_scan(bt, u_t, ops, h0_re, h0_im):
    w, m, cp, are, aim = ops
    tc, rows, d = u_t.shape
    nk = rows // bt
    nbs = min(bt, S5_SEQS_PER_STEP if nk > 1 else bt)
    nsteps = bt // nbs
    gp = w.shape[0]
    p = are.shape[-1] // 2
    pairs = LANES // (2 * SSM_GROUP_CH)
    wide = 2 * tc * SSM_GROUP_CH
    rblk = nbs * nk

    def pair(x):
        return x.astype(F32).reshape(nsteps, nbs, gp, 2 * p).transpose(2, 0, 1, 3)

    h0 = jnp.concatenate([pair(h0_re), pair(h0_im)], axis=-1)
    y_t, hl = pl.pallas_call(
        functools.partial(_s5_kernel, nk=nk, nb=nbs),
        out_shape=(jax.ShapeDtypeStruct((tc, rows, d), BF16), jax.ShapeDtypeStruct((gp, nsteps, nbs, 4 * p), F32)),
        grid=(gp // pairs, nsteps),
        in_specs=[pl.BlockSpec((tc, rblk, LANES), lambda i, j: (0, j, i)),
                  pl.BlockSpec((pairs, wide, 4 * p), lambda i, j: (i, 0, 0)),
                  pl.BlockSpec((pairs, 2, wide // 2, wide // 2), lambda i, j: (i, 0, 0, 0)),
                  pl.BlockSpec((pairs, 2, 4 * p, wide // 2), lambda i, j: (i, 0, 0, 0)),
                  pl.BlockSpec((pairs, 1, 2 * p), lambda i, j: (i, 0, 0)),
                  pl.BlockSpec((pairs, 1, 2 * p), lambda i, j: (i, 0, 0)),
                  pl.BlockSpec((pairs, 1, nbs, 4 * p), lambda i, j: (i, j, 0, 0))],
        out_specs=(pl.BlockSpec((tc, rblk, LANES), lambda i, j: (0, j, i)),
                   pl.BlockSpec((pairs, 1, nbs, 4 * p), lambda i, j: (i, j, 0, 0))),
        scratch_shapes=[pltpu.VMEM((rblk, wide), BF16)] + [pltpu.VMEM((rblk, 2 * p), F32)] * 4
        + [pltpu.VMEM((pairs, rblk, wide), BF16)],
        compiler_params=_params("parallel", "parallel"),
        name="s5_scan",
    )(u_t, w, m, cp, are, aim, h0)

    def unpair(x):
        return x.transpose(1, 2, 0, 3).reshape(bt, 2 * gp, p)

    return y_t, unpair(hl[..., :2 * p]), unpair(hl[..., 2 * p:])


def _row_permutation(outer, inner):
    n = outer * inner
    assert inner & (inner - 1) == 0
    row = lax.broadcasted_iota(jnp.int32, (n, n), 0)
    col = lax.broadcasted_iota(jnp.int32, (n, n), 1)
    src = (row & (inner - 1)) * outer + (row >> (inner.bit_length() - 1))
    return jnp.where(col == src, 1.0, 0.0).astype(BF16)


def _in_proj_kernel(x_ref, nw_ref, sh_ref, sc_ref, w_ref, o_ref):
    tc, rl = o_ref.shape[0], o_ref.shape[1]
    xm = _norm_mod(x_ref[...], nw_ref[...], sh_ref[0], sc_ref[0])
    u = jnp.dot(xm.astype(BF16), w_ref[...], preferred_element_type=F32).astype(BF16)
    ut = jnp.dot(_row_permutation(tc, rl), u, preferred_element_type=F32).astype(o_ref.dtype)
    for t in range(tc):
        o_ref[t] = ut[t * rl:(t + 1) * rl]


def _s5_in_proj(grp, h, nw, shift, scale, w_bf):
    d = h.shape[-1]
    tc = S5_CHUNK
    if not grp.per_batch:
        u = _norm_linear(grp, h, nw, shift, scale, w_bf, "s5_in_proj")
        u3 = jnp.pad(u.reshape(grp.bt, grp.seq, d), ((0, 0), (tc - grp.seq, 0), (0, 0)))
        return u3.transpose(1, 0, 2)
    rl = grp.tm // tc
    sh, sh_spec = grp.mod(shift)
    sc, sc_spec = grp.mod(scale)
    return pl.pallas_call(
        _in_proj_kernel,
        out_shape=jax.ShapeDtypeStruct((tc, grp.t // tc, d), BF16),
        grid=(grp.steps,),
        in_specs=[grp.rows(d), _const((1, d)), sh_spec, sc_spec, _const((d, d))],
        out_specs=pl.BlockSpec((tc, rl, d), lambda i: (0, i, 0)),
        compiler_params=_params("parallel"),
        name="s5_in_proj",
    )(h, nw.reshape(1, d), sh, sc, w_bf)


def _glu_kernel(y_ref, u_ref, ds_ref, wa_ref, wb_ref, h_ref, g_ref, o_ref):
    z = y_ref[...].astype(F32) + ds_ref[...] * u_ref[...].astype(F32)
    g = jax.nn.gelu(z).astype(BF16)
    a = jnp.dot(g, wa_ref[...], preferred_element_type=F32)
    b = jnp.dot(g, wb_ref[...], preferred_element_type=F32)
    o_ref[...] = h_ref[...] + g_ref[0] * (a * jax.nn.sigmoid(b))


def _glu_chunk_kernel(y_ref, u_ref, ds_ref, wa_ref, wb_ref, h_ref, g_ref, o_ref):
    tc, rl, d = y_ref.shape
    z = (y_ref[...].reshape(tc * rl, d).astype(F32)
         + ds_ref[...] * u_ref[...].reshape(tc * rl, d).astype(F32))
    g = jax.nn.gelu(z).astype(BF16)
    g = jnp.dot(_row_permutation(rl, tc), g, preferred_element_type=F32).astype(BF16)
    a = jnp.dot(g, wa_ref[...], preferred_element_type=F32)
    b = jnp.dot(g, wb_ref[...], preferred_element_type=F32)
    o_ref[...] = h_ref[...] + g_ref[0] * (a * jax.nn.sigmoid(b))


def _glu(grp, y_t, u_t, d_skip, wa_bf, wb_bf, h, gate):
    d = h.shape[-1]
    tc = y_t.shape[0]
    g, g_spec = grp.mod(gate)
    if grp.per_batch:
        chunk_spec = pl.BlockSpec((tc, grp.tm // tc, d), lambda i: (0, i, 0))
        return pl.pallas_call(
            _glu_chunk_kernel,
            out_shape=jax.ShapeDtypeStruct((grp.t, d), F32),
            grid=(grp.steps,),
            in_specs=[chunk_spec, chunk_spec, _const((1, d)), _const((d, d)), _const((d, d)), grp.rows(d), g_spec],
            out_specs=grp.rows(d),
            compiler_params=_params("parallel"),
            name="s5_glu",
        )(y_t, u_t, d_skip.astype(F32).reshape(1, d), wa_bf, wb_bf, h, g)
    y = y_t.transpose(1, 0, 2)[:, tc - grp.seq:, :].reshape(grp.t, d)
    u = u_t.transpose(1, 0, 2)[:, tc - grp.seq:, :].reshape(grp.t, d)
    return pl.pallas_call(
        _glu_kernel,
        out_shape=jax.ShapeDtypeStruct((grp.t, d), F32),
        grid=(grp.steps,),
        in_specs=[grp.rows(d), grp.rows(d), _const((1, d)), _const((d, d)), _const((d, d)), grp.rows(d), g_spec],
        out_specs=grp.rows(d),
        compiler_params=_params("parallel"),
        name="s5_glu",
    )(y, u, d_skip.astype(F32).reshape(1, d), wa_bf, wb_bf, h, g)


def _router_kernel(x_ref, nw_ref, sh_ref, sc_ref, wr_ref, br_ref, xf_ref, route_ref):
    xm = _norm_mod(x_ref[...], nw_ref[...], sh_ref[0], sc_ref[0])
    xf_ref[...] = xm
    logits = jnp.dot(xm, wr_ref[...], preferred_element_type=F32, precision=HIGHEST) + br_ref[...]
    lane = lax.broadcasted_iota(jnp.int32, logits.shape, 1).astype(F32)
    big = float(ROUTE_LANES)
    is_g = lane < N_EGROUPS
    gmax = jnp.max(jnp.where(is_g, logits, NEG_INF), axis=-1, keepdims=True)
    gsum = jnp.sum(jnp.where(is_g, jnp.exp(logits - gmax), 0.0), axis=-1, keepdims=True)
    g_sel = jnp.min(jnp.where(is_g & (logits == gmax), lane, big), axis=-1, keepdims=True)
    g_gate = 1.0 / gsum
    lo = N_EGROUPS + EXPERTS_PER_GROUP * g_sel
    is_e = (lane >= lo) & (lane < lo + EXPERTS_PER_GROUP)
    emax = jnp.max(jnp.where(is_e, logits, NEG_INF), axis=-1, keepdims=True)
    ex = jnp.where(is_e, jnp.exp(logits - emax), -1.0)
    p1 = jnp.max(ex, axis=-1, keepdims=True)
    i1 = jnp.min(jnp.where(ex == p1, lane, big), axis=-1, keepdims=True)
    ex2 = jnp.where(lane == i1, -1.0, ex)
    p2 = jnp.max(ex2, axis=-1, keepdims=True)
    i2 = jnp.min(jnp.where(ex2 == p2, lane, big), axis=-1, keepdims=True)
    inv = g_gate / (p1 + p2)
    rec = jnp.where(lane == 0, p1 * inv, 0.0)
    rec = jnp.where(lane == 1, p2 * inv, rec)
    rec = jnp.where(lane == 2, i1 - N_EGROUPS, rec)
    rec = jnp.where(lane == 3, i2 - N_EGROUPS, rec)
    route_ref[...] = rec


def _router(grp, h, nw, shift, scale, wr, br):
    d = h.shape[-1]
    sh, sh_spec = grp.mod(shift)
    sc, sc_spec = grp.mod(scale)
    return pl.pallas_call(
        _router_kernel,
        out_shape=(jax.ShapeDtypeStruct((grp.t, d), F32), jax.ShapeDtypeStruct((grp.t, ROUTE_LANES), F32)),
        grid=(grp.steps,),
        in_specs=[grp.rows(d), _const((1, d)), sh_spec, sc_spec, _const((d, ROUTE_LANES)), _const((1, ROUTE_LANES))],
        out_specs=(grp.rows(d), grp.rows(ROUTE_LANES)),
        compiler_params=_params("parallel"),
        name="moe_router",
    )(h, nw.reshape(1, d), sh, sc, wr, br)


def _moe_plan(route, bm):
    t = route.shape[0]
    a = 2 * t
    blk = math.gcd(a, LANES)
    e = route[:, 2:4].astype(jnp.int32).reshape(a)
    oh = e[:, None] == jnp.arange(N_EXPERTS, dtype=jnp.int32)[None, :]
    tri = jnp.tril(jnp.ones((blk, blk), BF16))
    within = jnp.einsum('ij,bjk->bik', tri, oh.astype(BF16).reshape(a // blk, blk, N_EXPERTS),
                        preferred_element_type=F32)
    tot = within[:, -1, :]
    offs = jnp.cumsum(tot, axis=0) - tot
    rank = (within + offs[:, None, :]).reshape(a, N_EXPERTS)
    counts = (offs[-1] + tot[-1]).astype(jnp.int32)
    padded = (counts + bm - 1) // bm * bm
    pad_end = jnp.cumsum(padded)
    pad_start = pad_end - padded
    ohf = oh.astype(F32)
    dest = jnp.sum(ohf * (rank - 1.0 + pad_start.astype(F32)[None, :]), axis=-1).astype(jnp.int32)
    n_blocks = a // bm + N_EXPERTS
    first_row = jnp.arange(n_blocks, dtype=jnp.int32) * bm
    block_expert = jnp.minimum(jnp.sum((pad_end[None, :] <= first_row[:, None]).astype(jnp.int32), axis=-1),
                               N_EXPERTS - 1)
    block_live = (first_row < pad_end[-1]).astype(jnp.int32)
    return dest, block_expert, block_live, n_blocks


def _row_copies(tm, copy):
    def issue(i, carry):
        for u in range(ROW_UNROLL):
            copy(i * ROW_UNROLL + u, 0).start()
            copy(i * ROW_UNROLL + u, 1).start()
        return carry

    def drain(i, carry):
        for u in range(ROW_UNROLL):
            copy(i * ROW_UNROLL + u, 0).wait()
            copy(i * ROW_UNROLL + u, 1).wait()
        return carry

    lax.fori_loop(0, tm // ROW_UNROLL, issue, 0)
    lax.fori_loop(0, tm // ROW_UNROLL, drain, 0)


def _dispatch_kernel(dest_ref, x_ref, xb_in, xb_out, sem):
    del xb_in
    tm = x_ref.shape[0]

    def copy(r, k):
        dst = dest_ref[0, 0, 2 * r + k]
        return pltpu.make_async_copy(x_ref.at[pl.ds(r, 1)], xb_out.at[pl.ds(dst, 1)], sem)

    _row_copies(tm, copy)


def _dispatch(grp, xf, dest, n_rows):
    d = xf.shape[-1]
    tm = min(grp.tm, 256)
    steps = grp.t // tm
    return pl.pallas_call(
        _dispatch_kernel,
        out_shape=jax.ShapeDtypeStruct((n_rows, d), F32),
        grid=(steps,),
        in_specs=[pl.BlockSpec((1, 1, 2 * tm), lambda i: (i, 0, 0), memory_space=pltpu.SMEM),
                  pl.BlockSpec((tm, d), lambda i: (i, 0)),
                  pl.BlockSpec(memory_space=pl.ANY)],
        out_specs=pl.BlockSpec(memory_space=pl.ANY),
        scratch_shapes=[pltpu.SemaphoreType.DMA],
        input_output_aliases={2: 0},
        compiler_params=_params("arbitrary"),
        name="moe_dispatch",
    )(dest.reshape(steps, 1, 2 * tm), xf, jnp.zeros((n_rows, d), F32))


def _expert_kernel(be_ref, live_ref, x_ref, w1_ref, w3_ref, w2_ref, o_ref, w1_scr, w3_scr, w2_scr):
    i = pl.program_id(0)

    @pl.when(jnp.logical_or(i == 0, be_ref[i] != be_ref[jnp.maximum(i - 1, 0)]))
    def _():
        w1_scr[...] = w1_ref[0, 0].astype(BF16)
        w3_scr[...] = w3_ref[0, 0].astype(BF16)
        w2_scr[...] = w2_ref[0, 0].astype(BF16)

    @pl.when(live_ref[i] > 0)
    def _():
        x = x_ref[...].astype(BF16)
        a = jnp.dot(x, w1_scr[...], preferred_element_type=F32)
        b = jnp.dot(x, w3_scr[...], preferred_element_type=F32)
        hid = (a * jax.nn.sigmoid(a) * b).astype(BF16)
        o_ref[...] = jnp.dot(hid, w2_scr[...], preferred_element_type=F32)

    @pl.when(live_ref[i] == 0)
    def _():
        o_ref[...] = jnp.zeros_like(o_ref)


def _experts(xb, block_expert, block_live, bm, layer, w1, w3, w2):
    n_rows, d = xb.shape
    f = w1.shape[-1]
    return pl.pallas_call(
        _expert_kernel,
        out_shape=jax.ShapeDtypeStruct((n_rows, d), F32),
        grid_spec=pltpu.PrefetchScalarGridSpec(
            num_scalar_prefetch=2,
            grid=(n_rows // bm,),
            in_specs=[pl.BlockSpec((bm, d), lambda i, be, lv: (i, 0)),
                      pl.BlockSpec((1, 1, d, f), lambda i, be, lv: (layer, be[i], 0, 0)),
                      pl.BlockSpec((1, 1, d, f), lambda i, be, lv: (layer, be[i], 0, 0)),
                      pl.BlockSpec((1, 1, f, d), lambda i, be, lv: (layer, be[i], 0, 0))],
            out_specs=pl.BlockSpec((bm, d), lambda i, be, lv: (i, 0)),
            scratch_shapes=[pltpu.VMEM((d, f), BF16), pltpu.VMEM((d, f), BF16), pltpu.VMEM((f, d), BF16)]),
        compiler_params=_params("arbitrary"),
        name="moe_experts",
    )(block_expert, block_live, xb, w1, w3, w2)


def _combine_kernel(dest_ref, h_ref, g_ref, route_ref, yb_hbm, o_ref, gbuf, sem):
    tm = h_ref.shape[0]

    def copy(r, k):
        src = dest_ref[0, 0, 2 * r + k]
        return pltpu.make_async_copy(yb_hbm.at[pl.ds(src, 1)], gbuf.at[k, pl.ds(r, 1)], sem)

    _row_copies(tm, copy)
    rec = route_ref[...]
    o_ref[...] = h_ref[...] + g_ref[0] * (rec[:, 0:1] * gbuf[0] + rec[:, 1:2] * gbuf[1])


def _combine(grp, h, gate, route, dest, yb):
    d = h.shape[-1]
    tm = min(grp.tm, 256)
    steps = grp.t // tm
    if grp.per_batch:
        per = grp.seq // tm
        g, g_spec = gate[:, None, :], pl.BlockSpec((1, 1, d), lambda i: (i // per, 0, 0))
    else:
        g, g_spec = grp.mod(gate)
    return pl.pallas_call(
        _combine_kernel,
        out_shape=jax.ShapeDtypeStruct((grp.t, d), F32),
        grid=(steps,),
        in_specs=[pl.BlockSpec((1, 1, 2 * tm), lambda i: (i, 0, 0), memory_space=pltpu.SMEM),
                  pl.BlockSpec((tm, d), lambda i: (i, 0)),
                  g_spec,
                  pl.BlockSpec((tm, ROUTE_LANES), lambda i: (i, 0)),
                  pl.BlockSpec(memory_space=pl.ANY)],
        out_specs=pl.BlockSpec((tm, d), lambda i: (i, 0)),
        scratch_shapes=[pltpu.VMEM((2, tm, d), F32), pltpu.SemaphoreType.DMA],
        compiler_params=_params("arbitrary"),
        name="moe_combine",
    )(dest.reshape(steps, 1, 2 * tm), h, g, route, yb)


def _moe(grp, h, nw, shift, scale, gate, wr, br, layer, w1, w3, w2):
    bm = 256 if grp.t >= 2048 else 32
    xf, route = _router(grp, h, nw, shift, scale, wr, br)
    dest, block_expert, block_live, n_blocks = _moe_plan(route, bm)
    xb = _dispatch(grp, xf, dest, n_blocks * bm)
    yb = _experts(xb, block_expert, block_live, bm, layer, w1, w3, w2)
    return _combine(grp, h, gate, route, dest, yb)


def _attn_kernel(lam_ref, q_ref, k_ref, v_ref, qnw_ref, sub_ref, o_ref, qz_scr, m_scr, l_scr, acc_scr, kmax_scr,
                 *, out_scale, online):
    tq = q_ref.shape[1]
    tk = tq
    rc = min(ATTN_ROWS, 2 * tq)
    qi = pl.program_id(2)
    qn = _half_norm(q_ref[0].astype(F32), qnw_ref[...], HEAD_DIM ** -0.5 * LOG2E)
    low = lax.broadcasted_iota(jnp.int32, qn.shape, 1) < HEAD_DIM
    qz = jnp.concatenate([jnp.where(low, qn, 0.0), jnp.where(low, 0.0, qn)], axis=0).astype(BF16)
    qz_scr[...] = qz
    l_scr[...] = jnp.zeros_like(l_scr)
    acc_scr[...] = jnp.zeros_like(acc_scr)
    if online:
        m_scr[...] = jnp.full_like(m_scr, NEG_INF)
    else:
        @pl.when(qi == 0)
        def _():
            kf = k_ref[0].astype(F32)
            klow = lax.broadcasted_iota(jnp.int32, kf.shape, 1) < HEAD_DIM
            sq = kf * kf
            n1 = jnp.max(jnp.sum(jnp.where(klow, sq, 0.0), axis=-1, keepdims=True), axis=0, keepdims=True)
            n2 = jnp.max(jnp.sum(jnp.where(klow, 0.0, sq), axis=-1, keepdims=True), axis=0, keepdims=True)
            kmax_scr[0:1, :] = jnp.broadcast_to(jnp.sqrt(n1), (1, V_DIM))
            kmax_scr[1:2, :] = jnp.broadcast_to(jnp.sqrt(n2), (1, V_DIM))

        qf = qz.astype(F32)
        qnorm = jnp.sqrt(jnp.sum(qf * qf, axis=-1, keepdims=True))
        m_scr[0:tq, :] = qnorm[:tq] * kmax_scr[0:1, :]
        m_scr[tq:, :] = qnorm[tq:] * kmax_scr[1:2, :]

    def tile(j, masked):
        r0 = pl.multiple_of(j * tk, tk)
        for c in range(2 * tq // rc):
            rows = slice(c * rc, (c + 1) * rc)
            q0 = (c * rc) % tq
            nk = min(tk, q0 + rc) if masked else tk
            k = k_ref[0, pl.ds(r0, nk), :]
            v = v_ref[0, pl.ds(r0, nk), :]
            s = lax.dot_general(qz_scr[rows, :], k, (((1,), (1,)), ((), ())), preferred_element_type=F32)
            if masked:
                row = lax.broadcasted_iota(jnp.int32, (rc, nk), 0) + q0
                col = lax.broadcasted_iota(jnp.int32, (rc, nk), 1)
                s = jnp.where(col <= jnp.where(row >= tq, row - tq, row), s, NEG_INF)
            parts = [s[:, t * LANES:(t + 1) * LANES] for t in range(nk // LANES)]
            if online:
                m_old = m_scr[rows, :]
                mx = functools.reduce(jnp.maximum, parts)
                m_new = jnp.maximum(m_old, jnp.max(mx, axis=-1, keepdims=True))
                alpha = jnp.exp2(m_old - m_new)
                m_scr[rows, :] = m_new
            else:
                m_new = m_scr[rows, :]
            ps = [jnp.exp2(x - m_new) for x in parts]
            lsum = functools.reduce(lambda a, b: a + b, ps)
            pv = jnp.dot(jnp.concatenate(ps, axis=1).astype(BF16), v, preferred_element_type=F32)
            if online:
                l_scr[rows, :] = alpha * l_scr[rows, :] + lsum
                acc_scr[rows, :] = alpha * acc_scr[rows, :] + pv
            else:
                l_scr[rows, :] = l_scr[rows, :] + lsum
                acc_scr[rows, :] = acc_scr[rows, :] + pv

    def full(j, carry):
        tile(j, False)
        return carry

    lax.fori_loop(0, qi, full, 0)
    tile(qi, True)
    acc = acc_scr[...] / jnp.sum(l_scr[...], axis=-1, keepdims=True)
    o = acc[:tq] - lam_ref[0] * acc[tq:]
    o = o * lax.rsqrt(jnp.mean(o * o, axis=-1, keepdims=True) + RMS_EPS) * sub_ref[...] * out_scale
    o_ref[0] = o.astype(o_ref.dtype)


def _attn_prompt(grp, q, kb, vb, lam, qnw, knw, subw, out_scale):
    bt, seq, d = grp.bt, grp.seq, q.shape[-1]
    nh = d // V_DIM
    tq = min(ATTN_TILE, seq)
    nq = seq // tq

    def run(online):
        return pl.pallas_call(
            functools.partial(_attn_kernel, out_scale=out_scale, online=online),
            out_shape=jax.ShapeDtypeStruct((bt, seq, d), BF16),
            grid_spec=pltpu.PrefetchScalarGridSpec(
                num_scalar_prefetch=1,
                grid=(bt, nh, nq),
                in_specs=[pl.BlockSpec((1, tq, V_DIM), lambda b, h, i, lam: (b, i, h)),
                          pl.BlockSpec((1, seq, V_DIM), lambda b, h, i, lam: (b, 0, h)),
                          pl.BlockSpec((1, seq, V_DIM), lambda b, h, i, lam: (b, 0, h)),
                          pl.BlockSpec((1, V_DIM), lambda b, h, i, lam: (0, 0)),
                          pl.BlockSpec((1, V_DIM), lambda b, h, i, lam: (0, 0))],
                out_specs=pl.BlockSpec((1, tq, V_DIM), lambda b, h, i, lam: (b, i, h)),
                scratch_shapes=[pltpu.VMEM((2 * tq, V_DIM), BF16), pltpu.VMEM((2 * tq, LANES), F32),
                                pltpu.VMEM((2 * tq, LANES), F32), pltpu.VMEM((2 * tq, V_DIM), F32),
                                pltpu.VMEM((SUBLANES, V_DIM), F32)]),
            compiler_params=_params("parallel", "parallel", "arbitrary"),
            name="diff_attn_prompt_online" if online else "diff_attn_prompt",
        )(lam.reshape(1), q.reshape(bt, seq, d), kb.reshape(bt, seq, d), vb.reshape(bt, seq, d), qnw, subw)

    bound = (HEAD_DIM ** 0.5) * jnp.max(jnp.abs(qnw)) * jnp.max(jnp.abs(knw)) * LOG2E
    out = lax.cond(bound <= FIXED_SHIFT_MAX, lambda: run(False), lambda: run(True))
    return out.reshape(grp.t, d)


def _paged_kernel(pt_ref, lam_ref, q_ref, kn_ref, vn_ref, qnw_ref, sub_ref, ck_hbm, cv_hbm, o_ref,
                  kbuf, vbuf, sem, *, n_chunks, out_scale):
    b = pl.program_id(0)
    nb = pl.num_programs(0)
    n_slots, pps = kbuf.shape[0], kbuf.shape[1]
    t = q_ref.shape[1]
    nh = q_ref.shape[-1] // V_DIM
    page = kbuf.shape[2] // nh
    total = nb * n_chunks

    def copies(seq_i, chunk, slot):
        out = []
        for pg in range(pps):
            phys = pt_ref[seq_i, chunk * pps + pg]
            out.append(pltpu.make_async_copy(ck_hbm.at[phys], kbuf.at[slot, pg], sem.at[slot, 0]))
            out.append(pltpu.make_async_copy(cv_hbm.at[phys], vbuf.at[slot, pg], sem.at[slot, 1]))
        return out

    def start(idx, slot):
        seq_i = idx // n_chunks
        for cp in copies(seq_i, idx - seq_i * n_chunks, slot):
            cp.start()

    @pl.when(b == 0)
    def _():
        for idx in range(n_slots - 1):
            start(idx, idx)

    def heads(ref):
        return jnp.stack([ref[0, :, hd * V_DIM:(hd + 1) * V_DIM] for hd in range(nh)], axis=0)

    qn = _half_norm(heads(q_ref).astype(F32), qnw_ref[...], HEAD_DIM ** -0.5)
    low = lax.broadcasted_iota(jnp.int32, qn.shape, 2) < HEAD_DIM
    qz = jnp.concatenate([jnp.where(low, qn, 0.0), jnp.where(low, 0.0, qn)], axis=1)
    qzb = qz.astype(BF16)

    def update(state, s, pv):
        m_old, l_old, acc = state
        m_new = jnp.maximum(m_old, jnp.max(s, axis=-1, keepdims=True))
        corr = jnp.exp(m_old - m_new)
        p = jnp.exp(s - m_new)
        return m_new, l_old * corr + jnp.sum(p, axis=-1, keepdims=True), acc * corr + pv(p)

    def gather_heads(buf, slot):
        return jnp.stack([
            jnp.concatenate([buf[slot, pg, pl.ds(hd, page, stride=nh), :] for pg in range(pps)], axis=0)
            for hd in range(nh)], axis=0).astype(BF16)

    def chunk_step(c, state):
        it = b * n_chunks + c
        slot = lax.rem(it, n_slots)
        nxt = it + (n_slots - 1)

        @pl.when(nxt < total)
        def _():
            start(nxt, lax.rem(nxt, n_slots))

        for cp in copies(b, c, slot):
            cp.wait()
        kh = gather_heads(kbuf, slot)
        vh = gather_heads(vbuf, slot)
        s = jnp.einsum('hqd,hkd->hqk', qzb, kh, preferred_element_type=F32)
        return update(state, s, lambda p: jnp.einsum('hqk,hkd->hqd', p.astype(BF16), vh,
                                                     preferred_element_type=F32))

    init = (jnp.full((nh, 2 * t, 1), NEG_INF, F32), jnp.zeros((nh, 2 * t, 1), F32),
            jnp.zeros((nh, 2 * t, V_DIM), F32))
    state = lax.fori_loop(0, n_chunks, chunk_step, init)

    kn, vn = heads(kn_ref), heads(vn_ref)
    row = lax.broadcasted_iota(jnp.int32, (nh, 2 * t, t), 1)
    col = lax.broadcasted_iota(jnp.int32, (nh, 2 * t, t), 2)
    s = jnp.concatenate([jnp.sum(qz * kn[:, j:j + 1, :], axis=-1, keepdims=True) for j in range(t)], axis=-1)
    s = jnp.where(col <= jnp.where(row >= t, row - t, row), s, NEG_INF)

    def pv_new(p):
        out = p[:, :, 0:1] * vn[:, 0:1, :]
        for j in range(1, t):
            out = out + p[:, :, j:j + 1] * vn[:, j:j + 1, :]
        return out

    _, l_fin, acc = update(state, s, pv_new)
    acc = acc / l_fin
    o = acc[:, :t] - lam_ref[0] * acc[:, t:]
    o = o * lax.rsqrt(jnp.mean(o * o, axis=-1, keepdims=True) + RMS_EPS) * sub_ref[...] * out_scale
    for hd in range(nh):
        o_ref[0, :, hd * V_DIM:(hd + 1) * V_DIM] = o[hd].astype(o_ref.dtype)


def _attn_sample(grp, q, k_new, v_new, cache_k, cache_v, page_table, lam, qnw, subw, out_scale):
    bd, t, d = grp.bt, grp.seq, q.shape[-1]
    n_phys, page, nh, vd = cache_k.shape
    n_pages = page_table.shape[1]
    pps = math.gcd(PAGES_PER_STEP, n_pages)
    n_chunks = n_pages // pps
    assert n_chunks >= PAGE_SLOTS - 1
    ck = cache_k.reshape(n_phys, page * nh, vd)
    cv = cache_v.reshape(n_phys, page * nh, vd)
    seq_spec = pl.BlockSpec((1, t, d), lambda b, pt, lam: (b, 0, 0))
    out = pl.pallas_call(
        functools.partial(_paged_kernel, n_chunks=n_chunks, out_scale=out_scale),
        out_shape=jax.ShapeDtypeStruct((bd, t, d), BF16),
        grid_spec=pltpu.PrefetchScalarGridSpec(
            num_scalar_prefetch=2,
            grid=(bd,),
            in_specs=[seq_spec, seq_spec, seq_spec,
                      pl.BlockSpec((1, vd), lambda b, pt, lam: (0, 0)),
                      pl.BlockSpec((1, vd), lambda b, pt, lam: (0, 0)),
                      pl.BlockSpec(memory_space=pl.ANY),
                      pl.BlockSpec(memory_space=pl.ANY)],
            out_specs=seq_spec,
            scratch_shapes=[pltpu.VMEM((PAGE_SLOTS, pps, page * nh, vd), F32),
                            pltpu.VMEM((PAGE_SLOTS, pps, page * nh, vd), F32),
                            pltpu.SemaphoreType.DMA((PAGE_SLOTS, 2))]),
        compiler_params=_params("arbitrary"),
        name="diff_attn_paged",
    )(page_table.astype(jnp.int32), lam.reshape(1), q.reshape(bd, t, d), k_new.reshape(bd, t, d),
      v_new.reshape(bd, t, d), qnw, subw, ck, cv)
    return out.reshape(grp.t, d)


def _out_proj_kernel(o_ref, w_ref, h_ref, g_ref, y_ref):
    y_ref[...] = h_ref[...] + g_ref[0] * jnp.dot(o_ref[...], w_ref[...], preferred_element_type=F32)


def _out_proj(grp, o, w_bf, h, gate):
    d = h.shape[-1]
    g, g_spec = grp.mod(gate)
    return pl.pallas_call(
        _out_proj_kernel,
        out_shape=jax.ShapeDtypeStruct((grp.t, d), F32),
        grid=(grp.steps,),
        in_specs=[grp.rows(d), _const((d, d)), grp.rows(d), g_spec],
        out_specs=grp.rows(d),
        compiler_params=_params("parallel"),
        name="attn_out_proj",
    )(o, w_bf, h, g)


def _trunk(x, mods, mod_kv, h0_re, h0_im, kv_past, wts):
    bt, seq, d = x.shape
    grp = _Group(bt, seq)
    h = x.astype(F32).reshape(grp.t, d)
    depth = len(mods)
    n_a = depth // 2
    new_re, new_im = [], []
    k = v = kb = vb = None
    for l in range(depth):
        if l == n_a:
            k, v, kb, vb = _kv_proj(grp, h, wts['norm_kv_w'], mod_kv[:, :d], mod_kv[:, d:], wts['w_kv'],
                                    wts['k_norm_w'])
        sh1, sc1, g1, sh2, sc2, g2 = jnp.split(mods[l], 6, axis=-1)
        if l < n_a:
            assert seq % S5_CHUNK == 0 or seq < S5_CHUNK
            u_t = _s5_in_proj(grp, h, wts['norm_mix_w'][l], sh1, sc1, wts['ssm_w_in'][l])
            ops = _s5_operators(wts['ssm_a_re'][l], wts['ssm_a_im'][l], wts['ssm_log_dt'][l], wts['ssm_b_re'][l],
                                wts['ssm_b_im'][l], wts['ssm_c_re'][l], wts['ssm_c_im'][l], min(seq, S5_CHUNK))
            y_t, hl_re, hl_im = _s5_scan(bt, u_t, ops, h0_re[l], h0_im[l])
            new_re.append(hl_re)
            new_im.append(hl_im)
            h = _glu(grp, y_t, u_t, wts['ssm_d_skip'][l], wts['ssm_w_glu_a'][l], wts['ssm_w_glu_b'][l], h, g1)
        else:
            j = l - n_a
            q = _norm_linear(grp, h, wts['norm_mix_w'][l], sh1, sc1, wts['attn_w_q'][j], "attn_q_proj")
            lam_init = 0.8 - 0.6 * math.exp(-0.3 * l)
            f32 = F32
            lam = (jnp.exp(jnp.sum(wts['lambda_q1'][j].astype(f32) * wts['lambda_k1'][j].astype(f32)))
                   - jnp.exp(jnp.sum(wts['lambda_q2'][j].astype(f32) * wts['lambda_k2'][j].astype(f32))) + lam_init)
            qnw = jnp.tile(wts['q_norm_w'][j].astype(f32), V_DIM // HEAD_DIM).reshape(1, V_DIM)
            subw = wts['subln_w'][j].astype(f32).reshape(1, V_DIM)
            if kv_past is None:
                o = _attn_prompt(grp, q, kb, vb, lam, qnw, wts['k_norm_w'].astype(f32), subw, 1.0 - lam_init)
            else:
                o = _attn_sample(grp, q, k, v, kv_past[0], kv_past[1], kv_past[2], lam, qnw, subw, 1.0 - lam_init)
            h = _out_proj(grp, o, wts['attn_w_o'][j], h, g1)
        h = _moe(grp, h, wts['norm_ffn_w'][l], sh2, sc2, g2, wts['moe_wr'][l], wts['moe_br'][l],
                 l, wts['moe_w1'], wts['moe_w3'], wts['moe_w2'])
    nh = d // V_DIM
    return (h.reshape(bt, seq, d), k.reshape(bt, seq, nh, V_DIM), v.reshape(bt, seq, nh, V_DIM),
            jnp.stack(new_re), jnp.stack(new_im))


def kernel(x_prompt, x_sample, cache_k, cache_v, page_table, state_ssm_re, state_ssm_im, c_prompt, c_sample,
           w_mod, b_mod, norm_mix_w, norm_ffn_w,
           ssm_w_in, ssm_a_re, ssm_a_im, ssm_log_dt, ssm_b_re, ssm_b_im, ssm_c_re, ssm_c_im, ssm_d_skip, ssm_w_glu,
           w_mod_kv, b_mod_kv, norm_kv_w, w_kv, k_norm_w,
           attn_w_q, q_norm_w, lambda_q1, lambda_k1, lambda_q2, lambda_k2, subln_w, attn_w_o,
           moe_w_group, moe_b_group, moe_w_erouter, moe_b_erouter, moe_w1, moe_w3, moe_w2):
    d = x_prompt.shape[-1]
    depth = w_mod.shape[0]
    n_prompt = c_prompt.shape[0]
    c_all = jnp.concatenate([c_prompt, c_sample], axis=0).astype(F32)
    mods = [_mod_linear(c_all, w_mod, l, b_mod[l]) for l in range(depth)]
    mod_kv = _mod_linear(c_all, w_mod_kv[None], 0, b_mod_kv)
    pad = ROUTE_LANES - N_EGROUPS - N_EXPERTS
    wr = jnp.concatenate([moe_w_group, moe_w_erouter.transpose(0, 2, 1, 3).reshape(depth, d, N_EXPERTS),
                          jnp.zeros((depth, d, pad), F32)], axis=-1)
    br = jnp.concatenate([moe_b_group, moe_b_erouter.reshape(depth, N_EXPERTS),
                          jnp.zeros((depth, pad), F32)], axis=-1).reshape(depth, 1, ROUTE_LANES)
    wts = dict(norm_mix_w=norm_mix_w, norm_ffn_w=norm_ffn_w, norm_kv_w=norm_kv_w, k_norm_w=k_norm_w,
               ssm_w_in=ssm_w_in.astype(BF16), ssm_a_re=ssm_a_re, ssm_a_im=ssm_a_im, ssm_log_dt=ssm_log_dt,
               ssm_b_re=ssm_b_re, ssm_b_im=ssm_b_im, ssm_c_re=ssm_c_re, ssm_c_im=ssm_c_im, ssm_d_skip=ssm_d_skip,
               ssm_w_glu_a=ssm_w_glu[..., :d].astype(BF16), ssm_w_glu_b=ssm_w_glu[..., d:].astype(BF16),
               w_kv=w_kv.astype(BF16), attn_w_q=attn_w_q.astype(BF16), q_norm_w=q_norm_w,
               lambda_q1=lambda_q1, lambda_k1=lambda_k1, lambda_q2=lambda_q2, lambda_k2=lambda_k2,
               subln_w=subln_w, attn_w_o=attn_w_o.astype(BF16), moe_wr=wr, moe_br=br,
               moe_w1=moe_w1, moe_w3=moe_w3, moe_w2=moe_w2)
    zero = jnp.zeros((state_ssm_re.shape[0], n_prompt) + state_ssm_re.shape[2:], F32)
    y_p, k_p, v_p, re_p, im_p = _trunk(x_prompt, [m[:n_prompt] for m in mods], mod_kv[:n_prompt],
                                       zero, zero, None, wts)
    y_s, k_s, v_s, re_s, im_s = _trunk(x_sample, [m[n_prompt:] for m in mods], mod_kv[n_prompt:],
                                       state_ssm_re, state_ssm_im, (cache_k, cache_v, page_table), wts)
    return (y_p, y_s, k_p, v_p, k_s, v_s, re_p, im_p, re_s, im_s)
```

```python
import functools
import math

import jax
import jax.numpy as jnp
from jax import lax
from jax.experimental import pallas as pl
from jax.experimental.pallas import tpu as pltpu

F32 = jnp.float32
BF16 = jnp.bfloat16
HIGHEST = lax.Precision.HIGHEST

SSM_GROUP_CH = 16
HEAD_DIM = 64
V_DIM = 2 * HEAD_DIM
N_EGROUPS = 4
EXPERTS_PER_GROUP = 8
N_EXPERTS = N_EGROUPS * EXPERTS_PER_GROUP
RMS_EPS = 1e-6
NEG_INF = -1e30

LANES = 128
SUBLANES = 8
MXU_DIM = 256

S5_CHUNK = MXU_DIM // SSM_GROUP_CH
S5_SEQS_PER_STEP = 4
ROW_TILE = 512
ATTN_TILE = 512
ATTN_ROWS = 512
LOG2E = 1.4426950408889634
FIXED_SHIFT_MAX = 60.0
ROW_UNROLL = 8
PAGE_SLOTS = 3
PAGES_PER_STEP = 4
ROUTE_LANES = LANES
VMEM_LIMIT = 48 * 1024 * 1024


def _params(*sem):
    return pltpu.CompilerParams(dimension_semantics=sem, vmem_limit_bytes=VMEM_LIMIT)


def _norm_mod(x, nw, shift, scale):
    ms = jnp.mean(x * x, axis=-1, keepdims=True)
    return x * lax.rsqrt(ms + RMS_EPS) * nw * (1.0 + scale) + shift


def _half_norm(x, w, scale):
    lane = lax.broadcasted_iota(jnp.int32, x.shape, x.ndim - 1)
    low = lane < HEAD_DIM
    sq = x * x
    tot = jnp.sum(sq, axis=-1, keepdims=True)
    lo = jnp.sum(jnp.where(low, sq, 0.0), axis=-1, keepdims=True)
    ms = jnp.where(low, lo, tot - lo) * (1.0 / HEAD_DIM)
    return x * lax.rsqrt(ms + RMS_EPS) * w * scale


class _Group:
    def __init__(self, bt, seq):
        self.bt, self.seq, self.t = bt, seq, bt * seq
        if seq % ROW_TILE == 0:
            self.tm, self.per_batch = ROW_TILE, True
        else:
            assert self.t <= ROW_TILE and self.t % SUBLANES == 0
            self.tm, self.per_batch = self.t, False
        self.steps = self.t // self.tm

    def mod(self, vec):
        return self.mod_rows(vec, self.tm)

    def mod_rows(self, vec, tm):
        d = vec.shape[-1]
        if self.per_batch:
            per = self.seq // tm
            return vec[:, None, :], pl.BlockSpec((1, 1, d), lambda i: (i // per, 0, 0))
        assert tm == self.tm
        arr = jnp.repeat(vec, self.seq, axis=0).reshape(self.steps, self.tm, d)
        return arr, pl.BlockSpec((1, self.tm, d), lambda i: (i, 0, 0))

    def rows(self, d):
        return pl.BlockSpec((self.tm, d), lambda i: (i, 0))


def _const(shape):
    return pl.BlockSpec(shape, lambda i: (0,) * len(shape))


def _mod_kernel(c_ref, w_ref, b_ref, o_ref):
    c = c_ref[...]
    a = c * jax.nn.sigmoid(c)
    o_ref[...] = jnp.dot(a, w_ref[0], preferred_element_type=F32, precision=HIGHEST) + b_ref[...]


def _mod_linear(c, w, layer, b):
    m, d = c.shape
    n = w.shape[-1]
    tn = 1024
    return pl.pallas_call(
        _mod_kernel,
        out_shape=jax.ShapeDtypeStruct((m, n), F32),
        grid=(n // tn,),
        in_specs=[pl.BlockSpec((m, d), lambda j: (0, 0)),
                  pl.BlockSpec((1, d, tn), lambda j: (layer, 0, j)),
                  pl.BlockSpec((1, tn), lambda j: (0, j))],
        out_specs=pl.BlockSpec((m, tn), lambda j: (0, j)),
        compiler_params=_params("parallel"),
        name="adaln_mod",
    )(c, w, b.reshape(1, n))


def _norm_linear_kernel(x_ref, nw_ref, sh_ref, sc_ref, w_ref, o_ref):
    xm = _norm_mod(x_ref[...], nw_ref[...], sh_ref[0], sc_ref[0])
    o_ref[...] = jnp.dot(xm.astype(BF16), w_ref[...], preferred_element_type=F32).astype(o_ref.dtype)


def _norm_linear(grp, h, nw, shift, scale, w_bf, name):
    d, n = w_bf.shape
    sh, sh_spec = grp.mod(shift)
    sc, sc_spec = grp.mod(scale)
    return pl.pallas_call(
        _norm_linear_kernel,
        out_shape=jax.ShapeDtypeStruct((grp.t, n), BF16),
        grid=(grp.steps,),
        in_specs=[grp.rows(d), _const((1, d)), sh_spec, sc_spec, _const((d, n))],
        out_specs=grp.rows(n),
        compiler_params=_params("parallel"),
        name=name,
    )(h, nw.reshape(1, d), sh, sc, w_bf)


def _kv_kernel(x_ref, nw_ref, sh_ref, sc_ref, w_ref, knw_ref, k_ref, v_ref, kb_ref, vb_ref):
    d = x_ref.shape[-1]
    xm = _norm_mod(x_ref[...], nw_ref[...], sh_ref[0], sc_ref[0])
    kv = jnp.dot(xm.astype(BF16), w_ref[...], preferred_element_type=F32)
    v = kv[:, d:]
    v_ref[...] = v
    vb_ref[...] = v.astype(BF16)
    for hd in range(d // V_DIM):
        sl = slice(hd * V_DIM, (hd + 1) * V_DIM)
        kn = _half_norm(kv[:, sl], knw_ref[...], 1.0)
        k_ref[:, sl] = kn
        kb_ref[:, sl] = kn.astype(BF16)


def _kv_proj(grp, h, nw, shift, scale, w_bf, k_norm_w):
    d = h.shape[-1]
    sh, sh_spec = grp.mod(shift)
    sc, sc_spec = grp.mod(scale)
    knw = jnp.tile(k_norm_w.astype(F32), V_DIM // HEAD_DIM).reshape(1, V_DIM)
    out = jax.ShapeDtypeStruct((grp.t, d), F32)
    outb = jax.ShapeDtypeStruct((grp.t, d), BF16)
    return pl.pallas_call(
        _kv_kernel,
        out_shape=(out, out, outb, outb),
        grid=(grp.steps,),
        in_specs=[grp.rows(d), _const((1, d)), sh_spec, sc_spec, _const((d, 2 * d)), _const((1, V_DIM))],
        out_specs=(grp.rows(d),) * 4,
        compiler_params=_params("parallel"),
        name="shared_kv",
    )(h, nw.reshape(1, d), sh, sc, w_bf, knw)


def _s5_operators(a_re, a_im, log_dt, b_re, b_im, c_re, c_im, n_valid):
    f32 = F32
    g, p = a_re.shape
    c = b_re.shape[-1]
    tc = S5_CHUNK
    a_re, a_im = a_re.astype(f32), a_im.astype(f32)
    dt = jnp.exp(log_dt.astype(f32))
    mag = jnp.exp(a_re * dt)
    lb_re, lb_im = mag * jnp.cos(a_im * dt), mag * jnp.sin(a_im * dt)
    den = a_re * a_re + a_im * a_im
    q_re = ((lb_re - 1.0) * a_re + lb_im * a_im) / den
    q_im = (lb_im * a_re - (lb_re - 1.0) * a_im) / den
    bb_re = q_re[..., None] * b_re - q_im[..., None] * b_im
    bb_im = q_re[..., None] * b_im + q_im[..., None] * b_re
    pw_re, pw_im = [jnp.ones_like(lb_re)], [jnp.zeros_like(lb_im)]
    for _ in range(tc):
        r, i = pw_re[-1], pw_im[-1]
        pw_re.append(r * lb_re - i * lb_im)
        pw_im.append(r * lb_im + i * lb_re)
    pw_re, pw_im = jnp.stack(pw_re), jnp.stack(pw_im)
    e_re = pw_re[:tc, :, :, None] * bb_re[None] - pw_im[:tc, :, :, None] * bb_im[None]
    e_im = pw_re[:tc, :, :, None] * bb_im[None] + pw_im[:tc, :, :, None] * bb_re[None]
    kt = (jnp.einsum('gcp,tgpd->tgdc', c_re, e_re, precision=HIGHEST)
          - jnp.einsum('gcp,tgpd->tgdc', c_im, e_im, precision=HIGHEST))
    s_idx = jnp.arange(tc)
    r0 = kt.transpose(1, 2, 0, 3).reshape(g, c, tc * c)
    m = jnp.stack([jnp.pad(r0, ((0, 0), (0, 0), (s * c, 0)))[:, :, :tc * c] for s in range(tc)], axis=1)
    m = m.reshape(g // 2, 2, tc * c, tc * c)
    w_re = e_re[::-1].transpose(1, 0, 3, 2).reshape(g // 2, 2, tc * c, p)
    w_im = e_im[::-1].transpose(1, 0, 3, 2).reshape(g // 2, 2, tc * c, p)
    z = jnp.zeros_like(w_re[:, 0])
    w = jnp.concatenate([
        jnp.concatenate([w_re[:, 0], z, w_im[:, 0], z], axis=-1),
        jnp.concatenate([z, w_re[:, 1], z, w_im[:, 1]], axis=-1)], axis=1)
    off = tc - n_valid
    tau = jnp.clip(s_idx - off + 1, 0, tc)
    live = (s_idx >= off)[:, None, None, None]
    d_re = jnp.where(live, c_re[None] * pw_re[tau][:, :, None, :] - c_im[None] * pw_im[tau][:, :, None, :], 0.0)
    d_im = jnp.where(live, c_re[None] * pw_im[tau][:, :, None, :] + c_im[None] * pw_re[tau][:, :, None, :], 0.0)
    cp_re = d_re.transpose(1, 3, 0, 2).reshape(g // 2, 2, p, tc * c)
    cp_im = -d_im.transpose(1, 3, 0, 2).reshape(g // 2, 2, p, tc * c)
    zc = jnp.zeros_like(cp_re[:, 0])
    cp = jnp.stack([
        jnp.concatenate([cp_re[:, 0], zc, cp_im[:, 0], zc], axis=1),
        jnp.concatenate([zc, cp_re[:, 1], zc, cp_im[:, 1]], axis=1)], axis=1)
    are = pw_re[n_valid].reshape(g // 2, 1, 2 * p)
    aim = pw_im[n_valid].reshape(g // 2, 1, 2 * p)
    return w.astype(BF16), m.astype(BF16), cp.astype(BF16), are, aim


def _move_lanes(x, src, dst):
    shift = (dst - src) % LANES
    return pltpu.roll(x, shift, axis=1) if shift else x


def _s5_kernel(u_ref, w_ref, m_ref, cp_ref, are_ref, aim_ref, h0_ref, y_ref, hl_ref,
               u_scr, sre_scr, sim_scr, hre_scr, him_scr, y_scr, *, nk, nb):
    tc = u_ref.shape[0]
    c = SSM_GROUP_CH
    per_tile = LANES // c
    half = sre_scr.shape[-1]
    width = tc * c
    n_pairs = w_ref.shape[0]
    lane = lax.broadcasted_iota(jnp.int32, (u_ref.shape[1] // 2, LANES), 1)

    for pi in range(n_pairs):
        for gi in range(2):
            for j in range(tc // per_tile):
                tile = jnp.zeros(lane.shape, jnp.int32)
                for tt in range(per_tile):
                    src = pltpu.bitcast(u_ref[j * per_tile + tt], jnp.int32)
                    moved = _move_lanes(src, (2 * pi + gi) * c, tt * c)
                    tile = jnp.where((lane >= tt * c) & (lane < (tt + 1) * c), moved, tile)
                lo = gi * width + j * LANES
                u_scr[:, lo:lo + LANES] = pltpu.bitcast(tile, BF16)
        s = jnp.dot(u_scr[...], w_ref[pi], preferred_element_type=F32)
        sre_scr[...] = s[:, :half]
        sim_scr[...] = s[:, half:]
        a_re, a_im = are_ref[pi], aim_ref[pi]
        h0 = h0_ref[pi, 0]

        def step(k, carry):
            h_re, h_im = carry
            rows = pl.ds(k, nb, stride=nk)
            hre_scr[rows, :] = h_re
            him_scr[rows, :] = h_im
            return (a_re * h_re - a_im * h_im + sre_scr[rows, :],
                    a_re * h_im + a_im * h_re + sim_scr[rows, :])

        h_re, h_im = lax.fori_loop(0, nk, step, (h0[:, :half], h0[:, half:]))
        hl_ref[pi, 0, :, :half] = h_re
        hl_ref[pi, 0, :, half:] = h_im
        hin = jnp.concatenate([hre_scr[...], him_scr[...]], axis=1).astype(BF16)
        for gi in range(2):
            sl = slice(gi * width, (gi + 1) * width)
            y = (jnp.dot(u_scr[:, sl], m_ref[pi, gi], preferred_element_type=F32)
                 + jnp.dot(hin, cp_ref[pi, gi], preferred_element_type=F32))
            y_scr[pi, :, sl] = y.astype(y_scr.dtype)

    for t in range(tc):
        tile = jnp.zeros(lane.shape, jnp.int32)
        for pi in range(n_pairs):
            for gi in range(2):
                lo = gi * width + (t // per_tile) * LANES
                src = pltpu.bitcast(y_scr[pi, :, lo:lo + LANES], jnp.int32)
                dst = (2 * pi + gi) * c
                moved = _move_lanes(src, (t % per_tile) * c, dst)
                tile = jnp.where((lane >= dst) & (lane < dst + c), moved, tile)
        y_ref[t] = pltpu.bitcast(tile, BF16)


def _s5_scan(bt, u_t, ops, h0_re, h0_im):
    w, m, cp, are, aim = ops
    tc, rows, d = u_t.shape
    nk = rows // bt
    nbs = min(bt, S5_SEQS_PER_STEP if nk > 1 else bt)
    nsteps = bt // nbs
    gp = w.shape[0]
    p = are.shape[-1] // 2
    pairs = LANES // (2 * SSM_GROUP_CH)
    wide = 2 * tc * SSM_GROUP_CH
    rblk = nbs * nk

    def pair(x):
        return x.astype(F32).reshape(nsteps, nbs, gp, 2 * p).transpose(2, 0, 1, 3)

    h0 = jnp.concatenate([pair(h0_re), pair(h0_im)], axis=-1)
    y_t, hl = pl.pallas_call(
        functools.partial(_s5_kernel, nk=nk, nb=nbs),
        out_shape=(jax.ShapeDtypeStruct((tc, rows, d), BF16), jax.ShapeDtypeStruct((gp, nsteps, nbs, 4 * p), F32)),
        grid=(gp // pairs, nsteps),
        in_specs=[pl.BlockSpec((tc, rblk, LANES), lambda i, j: (0, j, i)),
                  pl.BlockSpec((pairs, wide, 4 * p), lambda i, j: (i, 0, 0)),
                  pl.BlockSpec((pairs, 2, wide // 2, wide // 2), lambda i, j: (i, 0, 0, 0)),
                  pl.BlockSpec((pairs, 2, 4 * p, wide // 2), lambda i, j: (i, 0, 0, 0)),
                  pl.BlockSpec((pairs, 1, 2 * p), lambda i, j: (i, 0, 0)),
                  pl.BlockSpec((pairs, 1, 2 * p), lambda i, j: (i, 0, 0)),
                  pl.BlockSpec((pairs, 1, nbs, 4 * p), lambda i, j: (i, j, 0, 0))],
        out_specs=(pl.BlockSpec((tc, rblk, LANES), lambda i, j: (0, j, i)),
                   pl.BlockSpec((pairs, 1, nbs, 4 * p), lambda i, j: (i, j, 0, 0))),
        scratch_shapes=[pltpu.VMEM((rblk, wide), BF16)] + [pltpu.VMEM((rblk, 2 * p), F32)] * 4
        + [pltpu.VMEM((pairs, rblk, wide), BF16)],
        compiler_params=_params("parallel", "parallel"),
        name="s5_scan",
    )(u_t, w, m, cp, are, aim, h0)

    def unpair(x):
        return x.transpose(1, 2, 0, 3).reshape(bt, 2 * gp, p)

    return y_t, unpair(hl[..., :2 * p]), unpair(hl[..., 2 * p:])


def _row_permutation(outer, inner):
    n = outer * inner
    assert inner & (inner - 1) == 0
    row = lax.broadcasted_iota(jnp.int32, (n, n), 0)
    col = lax.broadcasted_iota(jnp.int32, (n, n), 1)
    src = (row & (inner - 1)) * outer + (row >> (inner.bit_length() - 1))
    return jnp.where(col == src, 1.0, 0.0).astype(BF16)


def _in_proj_kernel(x_ref, nw_ref, sh_ref, sc_ref, w_ref, o_ref):
    tc, rl = o_ref.shape[0], o_ref.shape[1]
    xm = _norm_mod(x_ref[...], nw_ref[...], sh_ref[0], sc_ref[0])
    u = jnp.dot(xm.astype(BF16), w_ref[...], preferred_element_type=F32).astype(BF16)
    ut = jnp.dot(_row_permutation(tc, rl), u, preferred_element_type=F32).astype(o_ref.dtype)
    for t in range(tc):
        o_ref[t] = ut[t * rl:(t + 1) * rl]


def _s5_in_proj(grp, h, nw, shift, scale, w_bf):
    d = h.shape[-1]
    tc = S5_CHUNK
    if not grp.per_batch:
        u = _norm_linear(grp, h, nw, shift, scale, w_bf, "s5_in_proj")
        u3 = jnp.pad(u.reshape(grp.bt, grp.seq, d), ((0, 0), (tc - grp.seq, 0), (0, 0)))
        return u3.transpose(1, 0, 2)
    rl = grp.tm // tc
    sh, sh_spec = grp.mod(shift)
    sc, sc_spec = grp.mod(scale)
    return pl.pallas_call(
        _in_proj_kernel,
        out_shape=jax.ShapeDtypeStruct((tc, grp.t // tc, d), BF16),
        grid=(grp.steps,),
        in_specs=[grp.rows(d), _const((1, d)), sh_spec, sc_spec, _const((d, d))],
        out_specs=pl.BlockSpec((tc, rl, d), lambda i: (0, i, 0)),
        compiler_params=_params("parallel"),
        name="s5_in_proj",
    )(h, nw.reshape(1, d), sh, sc, w_bf)


def _glu_kernel(y_ref, u_ref, ds_ref, wa_ref, wb_ref, h_ref, g_ref, o_ref):
    z = y_ref[...].astype(F32) + ds_ref[...] * u_ref[...].astype(F32)
    g = jax.nn.gelu(z).astype(BF16)
    a = jnp.dot(g, wa_ref[...], preferred_element_type=F32)
    b = jnp.dot(g, wb_ref[...], preferred_element_type=F32)
    o_ref[...] = h_ref[...] + g_ref[0] * (a * jax.nn.sigmoid(b))


def _glu_chunk_kernel(y_ref, u_ref, ds_ref, wa_ref, wb_ref, h_ref, g_ref, o_ref):
    tc, rl, d = y_ref.shape
    z = (y_ref[...].reshape(tc * rl, d).astype(F32)
         + ds_ref[...] * u_ref[...].reshape(tc * rl, d).astype(F32))
    g = jax.nn.gelu(z).astype(BF16)
    g = jnp.dot(_row_permutation(rl, tc), g, preferred_element_type=F32).astype(BF16)
    a = jnp.dot(g, wa_ref[...], preferred_element_type=F32)
    b = jnp.dot(g, wb_ref[...], preferred_element_type=F32)
    o_ref[...] = h_ref[...] + g_ref[0] * (a * jax.nn.sigmoid(b))


def _glu(grp, y_t, u_t, d_skip, wa_bf, wb_bf, h, gate):
    d = h.shape[-1]
    tc = y_t.shape[0]
    g, g_spec = grp.mod(gate)
    if grp.per_batch:
        chunk_spec = pl.BlockSpec((tc, grp.tm // tc, d), lambda i: (0, i, 0))
        return pl.pallas_call(
            _glu_chunk_kernel,
            out_shape=jax.ShapeDtypeStruct((grp.t, d), F32),
            grid=(grp.steps,),
            in_specs=[chunk_spec, chunk_spec, _const((1, d)), _const((d, d)), _const((d, d)), grp.rows(d), g_spec],
            out_specs=grp.rows(d),
            compiler_params=_params("parallel"),
            name="s5_glu",
        )(y_t, u_t, d_skip.astype(F32).reshape(1, d), wa_bf, wb_bf, h, g)
    y = y_t.transpose(1, 0, 2)[:, tc - grp.seq:, :].reshape(grp.t, d)
    u = u_t.transpose(1, 0, 2)[:, tc - grp.seq:, :].reshape(grp.t, d)
    return pl.pallas_call(
        _glu_kernel,
        out_shape=jax.ShapeDtypeStruct((grp.t, d), F32),
        grid=(grp.steps,),
        in_specs=[grp.rows(d), grp.rows(d), _const((1, d)), _const((d, d)), _const((d, d)), grp.rows(d), g_spec],
        out_specs=grp.rows(d),
        compiler_params=_params("parallel"),
        name="s5_glu",
    )(y, u, d_skip.astype(F32).reshape(1, d), wa_bf, wb_bf, h, g)


def _router_kernel(x_ref, nw_ref, sh_ref, sc_ref, wr_ref, br_ref, route_ref):
    xm = _norm_mod(x_ref[...], nw_ref[...], sh_ref[0], sc_ref[0])
    logits = jnp.dot(xm, wr_ref[...], preferred_element_type=F32, precision=HIGHEST) + br_ref[...]
    lane = lax.broadcasted_iota(jnp.int32, logits.shape, 1).astype(F32)
    big = float(ROUTE_LANES)
    is_g = lane < N_EGROUPS
    gmax = jnp.max(jnp.where(is_g, logits, NEG_INF), axis=-1, keepdims=True)
    gsum = jnp.sum(jnp.where(is_g, jnp.exp(logits - gmax), 0.0), axis=-1, keepdims=True)
    g_sel = jnp.min(jnp.where(is_g & (logits == gmax), lane, big), axis=-1, keepdims=True)
    g_gate = 1.0 / gsum
    lo = N_EGROUPS + EXPERTS_PER_GROUP * g_sel
    is_e = (lane >= lo) & (lane < lo + EXPERTS_PER_GROUP)
    emax = jnp.max(jnp.where(is_e, logits, NEG_INF), axis=-1, keepdims=True)
    ex = jnp.where(is_e, jnp.exp(logits - emax), -1.0)
    p1 = jnp.max(ex, axis=-1, keepdims=True)
    i1 = jnp.min(jnp.where(ex == p1, lane, big), axis=-1, keepdims=True)
    ex2 = jnp.where(lane == i1, -1.0, ex)
    p2 = jnp.max(ex2, axis=-1, keepdims=True)
    i2 = jnp.min(jnp.where(ex2 == p2, lane, big), axis=-1, keepdims=True)
    inv = g_gate / (p1 + p2)
    rec = jnp.where(lane == 0, p1 * inv, 0.0)
    rec = jnp.where(lane == 1, p2 * inv, rec)
    rec = jnp.where(lane == 2, i1 - N_EGROUPS, rec)
    rec = jnp.where(lane == 3, i2 - N_EGROUPS, rec)
    route_ref[...] = rec


def _router(grp, h, nw, shift, scale, wr, br):
    d = h.shape[-1]
    sh, sh_spec = grp.mod(shift)
    sc, sc_spec = grp.mod(scale)
    return pl.pallas_call(
        _router_kernel,
        out_shape=jax.ShapeDtypeStruct((grp.t, ROUTE_LANES), F32),
        grid=(grp.steps,),
        in_specs=[grp.rows(d), _const((1, d)), sh_spec, sc_spec, _const((d, ROUTE_LANES)), _const((1, ROUTE_LANES))],
        out_specs=grp.rows(ROUTE_LANES),
        compiler_params=_params("parallel"),
        name="moe_router",
    )(h, nw.reshape(1, d), sh, sc, wr, br)


def _moe_plan(route, bm):
    t = route.shape[0]
    a = 2 * t
    blk = math.gcd(a, LANES)
    e = route[:, 2:4].astype(jnp.int32).reshape(a)
    oh = e[:, None] == jnp.arange(N_EXPERTS, dtype=jnp.int32)[None, :]
    tri = jnp.tril(jnp.ones((blk, blk), BF16))
    within = jnp.einsum('ij,bjk->bik', tri, oh.astype(BF16).reshape(a // blk, blk, N_EXPERTS),
                        preferred_element_type=F32)
    tot = within[:, -1, :]
    offs = jnp.cumsum(tot, axis=0) - tot
    rank = (within + offs[:, None, :]).reshape(a, N_EXPERTS)
    counts = (offs[-1] + tot[-1]).astype(jnp.int32)
    padded = (counts + bm - 1) // bm * bm
    pad_end = jnp.cumsum(padded)
    pad_start = pad_end - padded
    ohf = oh.astype(F32)
    dest = jnp.sum(ohf * (rank - 1.0 + pad_start.astype(F32)[None, :]), axis=-1).astype(jnp.int32)
    n_blocks = a // bm + N_EXPERTS
    first_row = jnp.arange(n_blocks, dtype=jnp.int32) * bm
    block_expert = jnp.minimum(jnp.sum((pad_end[None, :] <= first_row[:, None]).astype(jnp.int32), axis=-1),
                               N_EXPERTS - 1)
    block_live = (first_row < pad_end[-1]).astype(jnp.int32)
    return dest, block_expert, block_live, n_blocks


def _row_copies(tm, copy):
    def issue(i, carry):
        for u in range(ROW_UNROLL):
            copy(i * ROW_UNROLL + u, 0).start()
            copy(i * ROW_UNROLL + u, 1).start()
        return carry

    def drain(i, carry):
        for u in range(ROW_UNROLL):
            copy(i * ROW_UNROLL + u, 0).wait()
            copy(i * ROW_UNROLL + u, 1).wait()
        return carry

    lax.fori_loop(0, tm // ROW_UNROLL, issue, 0)
    lax.fori_loop(0, tm // ROW_UNROLL, drain, 0)


def _dispatch_kernel(dest_ref, h_ref, nw_ref, sh_ref, sc_ref, xb_in, xb_out, x_scr, sem):
    del xb_in
    tm = h_ref.shape[0]
    x_scr[...] = _norm_mod(h_ref[...], nw_ref[...], sh_ref[0], sc_ref[0])

    def copy(r, k):
        dst = dest_ref[0, 0, 2 * r + k]
        return pltpu.make_async_copy(x_scr.at[pl.ds(r, 1)], xb_out.at[pl.ds(dst, 1)], sem)

    _row_copies(tm, copy)


def _dispatch(grp, h, nw, shift, scale, dest, xb):
    d = h.shape[-1]
    tm = min(grp.tm, 256)
    steps = grp.t // tm
    sh, sh_spec = grp.mod_rows(shift, tm)
    sc, sc_spec = grp.mod_rows(scale, tm)
    return pl.pallas_call(
        _dispatch_kernel,
        out_shape=jax.ShapeDtypeStruct(xb.shape, F32),
        grid=(steps,),
        in_specs=[pl.BlockSpec((1, 1, 2 * tm), lambda i: (i, 0, 0), memory_space=pltpu.SMEM),
                  pl.BlockSpec((tm, d), lambda i: (i, 0)),
                  _const((1, d)), sh_spec, sc_spec,
                  pl.BlockSpec(memory_space=pl.ANY)],
        out_specs=pl.BlockSpec(memory_space=pl.ANY),
        scratch_shapes=[pltpu.VMEM((tm, d), F32), pltpu.SemaphoreType.DMA],
        input_output_aliases={5: 0},
        compiler_params=_params("arbitrary"),
        name="moe_dispatch",
    )(dest.reshape(steps, 1, 2 * tm), h, nw.reshape(1, d), sh, sc, xb)


def _expert_kernel(be_ref, live_ref, x_ref, w1_ref, w3_ref, w2_ref, o_ref, w1_scr, w3_scr, w2_scr):
    i = pl.program_id(0)

    @pl.when(jnp.logical_or(i == 0, be_ref[i] != be_ref[jnp.maximum(i - 1, 0)]))
    def _():
        w1_scr[...] = w1_ref[0, 0].astype(BF16)
        w3_scr[...] = w3_ref[0, 0].astype(BF16)
        w2_scr[...] = w2_ref[0, 0].astype(BF16)

    @pl.when(live_ref[i] > 0)
    def _():
        x = x_ref[...].astype(BF16)
        a = jnp.dot(x, w1_scr[...], preferred_element_type=F32)
        b = jnp.dot(x, w3_scr[...], preferred_element_type=F32)
        hid = (a * jax.nn.sigmoid(a) * b).astype(BF16)
        o_ref[...] = jnp.dot(hid, w2_scr[...], preferred_element_type=F32)

    @pl.when(live_ref[i] == 0)
    def _():
        o_ref[...] = jnp.zeros_like(o_ref)


def _experts(xb, block_expert, block_live, bm, layer, w1, w3, w2):
    n_rows, d = xb.shape
    f = w1.shape[-1]
    return pl.pallas_call(
        _expert_kernel,
        out_shape=jax.ShapeDtypeStruct((n_rows, d), F32),
        grid_spec=pltpu.PrefetchScalarGridSpec(
            num_scalar_prefetch=2,
            grid=(n_rows // bm,),
            in_specs=[pl.BlockSpec((bm, d), lambda i, be, lv: (i, 0)),
                      pl.BlockSpec((1, 1, d, f), lambda i, be, lv: (layer, be[i], 0, 0)),
                      pl.BlockSpec((1, 1, d, f), lambda i, be, lv: (layer, be[i], 0, 0)),
                      pl.BlockSpec((1, 1, f, d), lambda i, be, lv: (layer, be[i], 0, 0))],
            out_specs=pl.BlockSpec((bm, d), lambda i, be, lv: (i, 0)),
            scratch_shapes=[pltpu.VMEM((d, f), BF16), pltpu.VMEM((d, f), BF16), pltpu.VMEM((f, d), BF16)]),
        compiler_params=_params("arbitrary"),
        name="moe_experts",
    )(block_expert, block_live, xb, w1, w3, w2)


def _combine_kernel(dest_ref, h_ref, g_ref, route_ref, yb_hbm, o_ref, gbuf, sem):
    tm = h_ref.shape[0]

    def copy(r, k):
        src = dest_ref[0, 0, 2 * r + k]
        return pltpu.make_async_copy(yb_hbm.at[pl.ds(src, 1)], gbuf.at[k, pl.ds(r, 1)], sem)

    _row_copies(tm, copy)
    rec = route_ref[...]
    o_ref[...] = h_ref[...] + g_ref[0] * (rec[:, 0:1] * gbuf[0] + rec[:, 1:2] * gbuf[1])


def _combine(grp, h, gate, route, dest, yb):
    d = h.shape[-1]
    tm = min(grp.tm, 256)
    steps = grp.t // tm
    g, g_spec = grp.mod_rows(gate, tm)
    return pl.pallas_call(
        _combine_kernel,
        out_shape=jax.ShapeDtypeStruct((grp.t, d), F32),
        grid=(steps,),
        in_specs=[pl.BlockSpec((1, 1, 2 * tm), lambda i: (i, 0, 0), memory_space=pltpu.SMEM),
                  pl.BlockSpec((tm, d), lambda i: (i, 0)),
                  g_spec,
                  pl.BlockSpec((tm, ROUTE_LANES), lambda i: (i, 0)),
                  pl.BlockSpec(memory_space=pl.ANY)],
        out_specs=pl.BlockSpec((tm, d), lambda i: (i, 0)),
        scratch_shapes=[pltpu.VMEM((2, tm, d), F32), pltpu.SemaphoreType.DMA],
        compiler_params=_params("arbitrary"),
        name="moe_combine",
    )(dest.reshape(steps, 1, 2 * tm), h, g, route, yb)


def _moe(grp, h, nw, shift, scale, gate, wr, br, layer, w1, w3, w2):
    bm = 256 if grp.t >= 2048 else 32
    route = _router(grp, h, nw, shift, scale, wr, br)
    dest, block_expert, block_live, n_blocks = _moe_plan(route, bm)
    xb = _dispatch(grp, h, nw, shift, scale, dest, jnp.zeros((n_blocks * bm, h.shape[-1]), F32))
    yb = _experts(xb, block_expert, block_live, bm, layer, w1, w3, w2)
    return _combine(grp, h, gate, route, dest, yb)


def _attn_kernel(lam_ref, q_ref, k_ref, v_ref, qnw_ref, sub_ref, o_ref, qz_scr, m_scr, l_scr, acc_scr, kmax_scr,
                 *, out_scale, online):
    tq = q_ref.shape[1]
    tk = tq
    rc = min(ATTN_ROWS, 2 * tq)
    qi = pl.program_id(2)
    qn = _half_norm(q_ref[0].astype(F32), qnw_ref[...], HEAD_DIM ** -0.5 * LOG2E)
    low = lax.broadcasted_iota(jnp.int32, qn.shape, 1) < HEAD_DIM
    qz = jnp.concatenate([jnp.where(low, qn, 0.0), jnp.where(low, 0.0, qn)], axis=0).astype(BF16)
    qz_scr[...] = qz
    l_scr[...] = jnp.zeros_like(l_scr)
    acc_scr[...] = jnp.zeros_like(acc_scr)
    if online:
        m_scr[...] = jnp.full_like(m_scr, NEG_INF)
    else:
        @pl.when(qi == 0)
        def _():
            kf = k_ref[0].astype(F32)
            klow = lax.broadcasted_iota(jnp.int32, kf.shape, 1) < HEAD_DIM
            sq = kf * kf
            n1 = jnp.max(jnp.sum(jnp.where(klow, sq, 0.0), axis=-1, keepdims=True), axis=0, keepdims=True)
            n2 = jnp.max(jnp.sum(jnp.where(klow, 0.0, sq), axis=-1, keepdims=True), axis=0, keepdims=True)
            qmax = jnp.max(jnp.abs(qnw_ref[...]), axis=-1, keepdims=True) * (LOG2E * (1.0 + 2.0 ** -7))
            kmax_scr[0:1, :] = jnp.broadcast_to(jnp.sqrt(n1) * qmax, (1, V_DIM))
            kmax_scr[1:2, :] = jnp.broadcast_to(jnp.sqrt(n2) * qmax, (1, V_DIM))

        m_scr[0:tq, :] = jnp.broadcast_to(kmax_scr[0:1, :], (tq, V_DIM))
        m_scr[tq:, :] = jnp.broadcast_to(kmax_scr[1:2, :], (tq, V_DIM))

    def tile(j, masked):
        r0 = pl.multiple_of(j * tk, tk)
        for c in range(2 * tq // rc):
            rows = slice(c * rc, (c + 1) * rc)
            q0 = (c * rc) % tq
            nk = min(tk, q0 + rc) if masked else tk
            k = k_ref[0, pl.ds(r0, nk), :]
            v = v_ref[0, pl.ds(r0, nk), :]
            s = lax.dot_general(qz_scr[rows, :], k, (((1,), (1,)), ((), ())), preferred_element_type=F32)
            if masked:
                row = lax.broadcasted_iota(jnp.int32, (rc, nk), 0) + q0
                col = lax.broadcasted_iota(jnp.int32, (rc, nk), 1)
                s = jnp.where(col <= jnp.where(row >= tq, row - tq, row), s, NEG_INF)
            parts = [s[:, t * LANES:(t + 1) * LANES] for t in range(nk // LANES)]
            if online:
                m_old = m_scr[rows, :]
                mx = functools.reduce(jnp.maximum, parts)
                m_new = jnp.maximum(m_old, jnp.max(mx, axis=-1, keepdims=True))
                alpha = jnp.exp2(m_old - m_new)
                m_scr[rows, :] = m_new
            else:
                m_new = m_scr[rows, :]
            ps = [jnp.exp2(x - m_new) for x in parts]
            lsum = functools.reduce(lambda a, b: a + b, ps)
            pv = jnp.dot(jnp.concatenate(ps, axis=1).astype(BF16), v, preferred_element_type=F32)
            if online:
                l_scr[rows, :] = alpha * l_scr[rows, :] + lsum
                acc_scr[rows, :] = alpha * acc_scr[rows, :] + pv
            else:
                l_scr[rows, :] = l_scr[rows, :] + lsum
                acc_scr[rows, :] = acc_scr[rows, :] + pv

    def full(j, carry):
        tile(j, False)
        return carry

    lax.fori_loop(0, qi, full, 0)
    tile(qi, True)
    acc = acc_scr[...] / jnp.sum(l_scr[...], axis=-1, keepdims=True)
    o = acc[:tq] - lam_ref[0] * acc[tq:]
    o = o * lax.rsqrt(jnp.mean(o * o, axis=-1, keepdims=True) + RMS_EPS) * sub_ref[...] * out_scale
    o_ref[0] = o.astype(o_ref.dtype)


def _attn_prompt(grp, q, kb, vb, lam, qnw, knw, subw, out_scale):
    bt, seq, d = grp.bt, grp.seq, q.shape[-1]
    nh = d // V_DIM
    tq = min(ATTN_TILE, seq)
    nq = seq // tq

    def run(online):
        return pl.pallas_call(
            functools.partial(_attn_kernel, out_scale=out_scale, online=online),
            out_shape=jax.ShapeDtypeStruct((bt, seq, d), BF16),
            grid_spec=pltpu.PrefetchScalarGridSpec(
                num_scalar_prefetch=1,
                grid=(bt, nh, nq),
                in_specs=[pl.BlockSpec((1, tq, V_DIM), lambda b, h, i, lam: (b, i, h)),
                          pl.BlockSpec((1, seq, V_DIM), lambda b, h, i, lam: (b, 0, h)),
                          pl.BlockSpec((1, seq, V_DIM), lambda b, h, i, lam: (b, 0, h)),
                          pl.BlockSpec((1, V_DIM), lambda b, h, i, lam: (0, 0)),
                          pl.BlockSpec((1, V_DIM), lambda b, h, i, lam: (0, 0))],
                out_specs=pl.BlockSpec((1, tq, V_DIM), lambda b, h, i, lam: (b, i, h)),
                scratch_shapes=[pltpu.VMEM((2 * tq, V_DIM), BF16), pltpu.VMEM((2 * tq, LANES), F32),
                                pltpu.VMEM((2 * tq, LANES), F32), pltpu.VMEM((2 * tq, V_DIM), F32),
                                pltpu.VMEM((SUBLANES, V_DIM), F32)]),
            compiler_params=_params("parallel", "parallel", "arbitrary"),
            name="diff_attn_prompt_online" if online else "diff_attn_prompt",
        )(lam.reshape(1), q.reshape(bt, seq, d), kb.reshape(bt, seq, d), vb.reshape(bt, seq, d), qnw, subw)

    bound = (HEAD_DIM ** 0.5) * jnp.max(jnp.abs(qnw)) * jnp.max(jnp.abs(knw)) * LOG2E
    out = lax.cond(bound <= FIXED_SHIFT_MAX, lambda: run(False), lambda: run(True))
    return out.reshape(grp.t, d)


def _paged_kernel(pt_ref, lam_ref, q_ref, kn_ref, vn_ref, qnw_ref, sub_ref, ck_hbm, cv_hbm, o_ref,
                  kbuf, vbuf, sem, *, n_chunks, out_scale):
    b = pl.program_id(0)
    nb = pl.num_programs(0)
    n_slots, pps = kbuf.shape[0], kbuf.shape[1]
    t = q_ref.shape[1]
    nh = q_ref.shape[-1] // V_DIM
    page = kbuf.shape[2] // nh
    total = nb * n_chunks

    def copies(seq_i, chunk, slot):
        out = []
        for pg in range(pps):
            phys = pt_ref[seq_i, chunk * pps + pg]
            out.append(pltpu.make_async_copy(ck_hbm.at[phys], kbuf.at[slot, pg], sem.at[slot, 0]))
            out.append(pltpu.make_async_copy(cv_hbm.at[phys], vbuf.at[slot, pg], sem.at[slot, 1]))
        return out

    def start(idx, slot):
        seq_i = idx // n_chunks
        for cp in copies(seq_i, idx - seq_i * n_chunks, slot):
            cp.start()

    @pl.when(b == 0)
    def _():
        for idx in range(n_slots - 1):
            start(idx, idx)

    def heads(ref):
        return jnp.stack([ref[0, :, hd * V_DIM:(hd + 1) * V_DIM] for hd in range(nh)], axis=0)

    qn = _half_norm(heads(q_ref).astype(F32), qnw_ref[...], HEAD_DIM ** -0.5)
    low = lax.broadcasted_iota(jnp.int32, qn.shape, 2) < HEAD_DIM
    qz = jnp.concatenate([jnp.where(low, qn, 0.0), jnp.where(low, 0.0, qn)], axis=1)
    qzb = qz.astype(BF16)

    def update(state, s, pv):
        m_old, l_old, acc = state
        m_new = jnp.maximum(m_old, jnp.max(s, axis=-1, keepdims=True))
        corr = jnp.exp(m_old - m_new)
        p = jnp.exp(s - m_new)
        return m_new, l_old * corr + jnp.sum(p, axis=-1, keepdims=True), acc * corr + pv(p)

    def gather_heads(buf, slot):
        return jnp.stack([
            jnp.concatenate([buf[slot, pg, pl.ds(hd, page, stride=nh), :] for pg in range(pps)], axis=0)
            for hd in range(nh)], axis=0).astype(BF16)

    def chunk_step(c, state):
        it = b * n_chunks + c
        slot = lax.rem(it, n_slots)
        nxt = it + (n_slots - 1)

        @pl.when(nxt < total)
        def _():
            start(nxt, lax.rem(nxt, n_slots))

        for cp in copies(b, c, slot):
            cp.wait()
        kh = gather_heads(kbuf, slot)
        vh = gather_heads(vbuf, slot)
        s = jnp.einsum('hqd,hkd->hqk', qzb, kh, preferred_element_type=F32)
        return update(state, s, lambda p: jnp.einsum('hqk,hkd->hqd', p.astype(BF16), vh,
                                                     preferred_element_type=F32))

    init = (jnp.full((nh, 2 * t, 1), NEG_INF, F32), jnp.zeros((nh, 2 * t, 1), F32),
            jnp.zeros((nh, 2 * t, V_DIM), F32))
    state = lax.fori_loop(0, n_chunks, chunk_step, init)

    kn, vn = heads(kn_ref), heads(vn_ref)
    row = lax.broadcasted_iota(jnp.int32, (nh, 2 * t, t), 1)
    col = lax.broadcasted_iota(jnp.int32, (nh, 2 * t, t), 2)
    s = jnp.concatenate([jnp.sum(qz * kn[:, j:j + 1, :], axis=-1, keepdims=True) for j in range(t)], axis=-1)
    s = jnp.where(col <= jnp.where(row >= t, row - t, row), s, NEG_INF)

    def pv_new(p):
        out = p[:, :, 0:1] * vn[:, 0:1, :]
        for j in range(1, t):
            out = out + p[:, :, j:j + 1] * vn[:, j:j + 1, :]
        return out

    _, l_fin, acc = update(state, s, pv_new)
    acc = acc / l_fin
    o = acc[:, :t] - lam_ref[0] * acc[:, t:]
    o = o * lax.rsqrt(jnp.mean(o * o, axis=-1, keepdims=True) + RMS_EPS) * sub_ref[...] * out_scale
    for hd in range(nh):
        o_ref[0, :, hd * V_DIM:(hd + 1) * V_DIM] = o[hd].astype(o_ref.dtype)


def _attn_sample(grp, q, k_new, v_new, cache_k, cache_v, page_table, lam, qnw, subw, out_scale):
    bd, t, d = grp.bt, grp.seq, q.shape[-1]
    n_phys, page, nh, vd = cache_k.shape
    n_pages = page_table.shape[1]
    pps = math.gcd(PAGES_PER_STEP, n_pages)
    n_chunks = n_pages // pps
    assert n_chunks >= PAGE_SLOTS - 1
    ck = cache_k.reshape(n_phys, page * nh, vd)
    cv = cache_v.reshape(n_phys, page * nh, vd)
    seq_spec = pl.BlockSpec((1, t, d), lambda b, pt, lam: (b, 0, 0))
    out = pl.pallas_call(
        functools.partial(_paged_kernel, n_chunks=n_chunks, out_scale=out_scale),
        out_shape=jax.ShapeDtypeStruct((bd, t, d), BF16),
        grid_spec=pltpu.PrefetchScalarGridSpec(
            num_scalar_prefetch=2,
            grid=(bd,),
            in_specs=[seq_spec, seq_spec, seq_spec,
                      pl.BlockSpec((1, vd), lambda b, pt, lam: (0, 0)),
                      pl.BlockSpec((1, vd), lambda b, pt, lam: (0, 0)),
                      pl.BlockSpec(memory_space=pl.ANY),
                      pl.BlockSpec(memory_space=pl.ANY)],
            out_specs=seq_spec,
            scratch_shapes=[pltpu.VMEM((PAGE_SLOTS, pps, page * nh, vd), F32),
                            pltpu.VMEM((PAGE_SLOTS, pps, page * nh, vd), F32),
                            pltpu.SemaphoreType.DMA((PAGE_SLOTS, 2))]),
        compiler_params=_params("arbitrary"),
        name="diff_attn_paged",
    )(page_table.astype(jnp.int32), lam.reshape(1), q.reshape(bd, t, d), k_new.reshape(bd, t, d),
      v_new.reshape(bd, t, d), qnw, subw, ck, cv)
    return out.reshape(grp.t, d)


def _out_proj_kernel(o_ref, w_ref, h_ref, g_ref, y_ref):
    y_ref[...] = h_ref[...] + g_ref[0] * jnp.dot(o_ref[...], w_ref[...], preferred_element_type=F32)


def _out_proj(grp, o, w_bf, h, gate):
    d = h.shape[-1]
    g, g_spec = grp.mod(gate)
    return pl.pallas_call(
        _out_proj_kernel,
        out_shape=jax.ShapeDtypeStruct((grp.t, d), F32),
        grid=(grp.steps,),
        in_specs=[grp.rows(d), _const((d, d)), grp.rows(d), g_spec],
        out_specs=grp.rows(d),
        compiler_params=_params("parallel"),
        name="attn_out_proj",
    )(o, w_bf, h, g)


def _trunk(x, mods, mod_kv, h0_re, h0_im, kv_past, wts):
    bt, seq, d = x.shape
    grp = _Group(bt, seq)
    h = x.astype(F32).reshape(grp.t, d)
    depth = len(mods)
    n_a = depth // 2
    new_re, new_im = [], []
    k = v = kb = vb = None
    for l in range(depth):
        if l == n_a:
            k, v, kb, vb = _kv_proj(grp, h, wts['norm_kv_w'], mod_kv[:, :d], mod_kv[:, d:], wts['w_kv'],
                                    wts['k_norm_w'])
        sh1, sc1, g1, sh2, sc2, g2 = jnp.split(mods[l], 6, axis=-1)
        if l < n_a:
            assert seq % S5_CHUNK == 0 or seq < S5_CHUNK
            u_t = _s5_in_proj(grp, h, wts['norm_mix_w'][l], sh1, sc1, wts['ssm_w_in'][l])
            ops = _s5_operators(wts['ssm_a_re'][l], wts['ssm_a_im'][l], wts['ssm_log_dt'][l], wts['ssm_b_re'][l],
                                wts['ssm_b_im'][l], wts['ssm_c_re'][l], wts['ssm_c_im'][l], min(seq, S5_CHUNK))
            y_t, hl_re, hl_im = _s5_scan(bt, u_t, ops, h0_re[l], h0_im[l])
            new_re.append(hl_re)
            new_im.append(hl_im)
            h = _glu(grp, y_t, u_t, wts['ssm_d_skip'][l], wts['ssm_w_glu_a'][l], wts['ssm_w_glu_b'][l], h, g1)
        else:
            j = l - n_a
            q = _norm_linear(grp, h, wts['norm_mix_w'][l], sh1, sc1, wts['attn_w_q'][j], "attn_q_proj")
            lam_init = 0.8 - 0.6 * math.exp(-0.3 * l)
            f32 = F32
            lam = (jnp.exp(jnp.sum(wts['lambda_q1'][j].astype(f32) * wts['lambda_k1'][j].astype(f32)))
                   - jnp.exp(jnp.sum(wts['lambda_q2'][j].astype(f32) * wts['lambda_k2'][j].astype(f32))) + lam_init)
            qnw = jnp.tile(wts['q_norm_w'][j].astype(f32), V_DIM // HEAD_DIM).reshape(1, V_DIM)
            subw = wts['subln_w'][j].astype(f32).reshape(1, V_DIM)
            if kv_past is None:
                o = _attn_prompt(grp, q, kb, vb, lam, qnw, wts['k_norm_w'].astype(f32), subw, 1.0 - lam_init)
            else:
                o = _attn_sample(grp, q, k, v, kv_past[0], kv_past[1], kv_past[2], lam, qnw, subw, 1.0 - lam_init)
            h = _out_proj(grp, o, wts['attn_w_o'][j], h, g1)
        h = _moe(grp, h, wts['norm_ffn_w'][l], sh2, sc2, g2, wts['moe_wr'][l], wts['moe_br'][l],
                 l, wts['moe_w1'], wts['moe_w3'], wts['moe_w2'])
    nh = d // V_DIM
    return (h.reshape(bt, seq, d), k.reshape(bt, seq, nh, V_DIM), v.reshape(bt, seq, nh, V_DIM),
            jnp.stack(new_re), jnp.stack(new_im))


def kernel(x_prompt, x_sample, cache_k, cache_v, page_table, state_ssm_re, state_ssm_im, c_prompt, c_sample,
           w_mod, b_mod, norm_mix_w, norm_ffn_w,
           ssm_w_in, ssm_a_re, ssm_a_im, ssm_log_dt, ssm_b_re, ssm_b_im, ssm_c_re, ssm_c_im, ssm_d_skip, ssm_w_glu,
           w_mod_kv, b_mod_kv, norm_kv_w, w_kv, k_norm_w,
           attn_w_q, q_norm_w, lambda_q1, lambda_k1, lambda_q2, lambda_k2, subln_w, attn_w_o,
           moe_w_group, moe_b_group, moe_w_erouter, moe_b_erouter, moe_w1, moe_w3, moe_w2):
    d = x_prompt.shape[-1]
    depth = w_mod.shape[0]
    n_prompt = c_prompt.shape[0]
    c_all = jnp.concatenate([c_prompt, c_sample], axis=0).astype(F32)
    mods = [_mod_linear(c_all, w_mod, l, b_mod[l]) for l in range(depth)]
    mod_kv = _mod_linear(c_all, w_mod_kv[None], 0, b_mod_kv)
    pad = ROUTE_LANES - N_EGROUPS - N_EXPERTS
    wr = jnp.concatenate([moe_w_group, moe_w_erouter.transpose(0, 2, 1, 3).reshape(depth, d, N_EXPERTS),
                          jnp.zeros((depth, d, pad), F32)], axis=-1)
    br = jnp.concatenate([moe_b_group, moe_b_erouter.reshape(depth, N_EXPERTS),
                          jnp.zeros((depth, pad), F32)], axis=-1).reshape(depth, 1, ROUTE_LANES)
    wts = dict(norm_mix_w=norm_mix_w, norm_ffn_w=norm_ffn_w, norm_kv_w=norm_kv_w, k_norm_w=k_norm_w,
               ssm_w_in=ssm_w_in.astype(BF16), ssm_a_re=ssm_a_re, ssm_a_im=ssm_a_im, ssm_log_dt=ssm_log_dt,
               ssm_b_re=ssm_b_re, ssm_b_im=ssm_b_im, ssm_c_re=ssm_c_re, ssm_c_im=ssm_c_im, ssm_d_skip=ssm_d_skip,
               ssm_w_glu_a=ssm_w_glu[..., :d].astype(BF16), ssm_w_glu_b=ssm_w_glu[..., d:].astype(BF16),
               w_kv=w_kv.astype(BF16), attn_w_q=attn_w_q.astype(BF16), q_norm_w=q_norm_w,
               lambda_q1=lambda_q1, lambda_k1=lambda_k1, lambda_q2=lambda_q2, lambda_k2=lambda_k2,
               subln_w=subln_w, attn_w_o=attn_w_o.astype(BF16), moe_wr=wr, moe_br=br,
               moe_w1=moe_w1, moe_w3=moe_w3, moe_w2=moe_w2)
    zero = jnp.zeros((state_ssm_re.shape[0], n_prompt) + state_ssm_re.shape[2:], F32)
    y_p, k_p, v_p, re_p, im_p = _trunk(x_prompt, [m[:n_prompt] for m in mods], mod_kv[:n_prompt],
                                       zero, zero, None, wts)
    y_s, k_s, v_s, re_s, im_s = _trunk(x_sample, [m[n_prompt:] for m in mods], mod_kv[n_prompt:],
                                       state_ssm_re, state_ssm_im, (cache_k, cache_v, page_table), wts)
    return (y_p, y_s, k_p, v_p, k_s, v_s, re_p, im_p, re_s, im_s)
```
